```python
import math
import jax
import jax.numpy as jnp
from jax import lax
import numpy as np

D_MODEL = 4096
BATCH = 2
SEQ = 8192
DEPTH = 2

GRID_W = 64
CTX_LEN = 256
N_BRANCH = 4
BRANCH_W = D_MODEL // 4
SHORT_K = 3
EPS = 1e-6
ALPHA = (2 * DEPTH) ** 0.25
BETA = (8 * DEPTH) ** -0.25

HY_W = BRANCH_W
HY_ORDER = 2
HY_BANDS = 16
HY_EMB = 2 * HY_BANDS + 1
HY_HID = 64
HY_TARGET = 1e-2
HY_FAST = 0.3
HY_SLOW = 1.5

SSD_W = BRANCH_W
SSD_HEADDIM = 64
SSD_HEADS = SSD_W // SSD_HEADDIM
SSD_GROUPS = 4
SSD_RPG = SSD_HEADS // SSD_GROUPS
SSD_STATE = 128
SSD_CHUNK = 128
SSD_XBC = SSD_W + 2 * SSD_GROUPS * SSD_STATE

GLA_HEADS = 4
GLA_DV = BRANCH_W
GLA_DK = BRANCH_W // 2
GLA_HDK = GLA_DK // GLA_HEADS
GLA_HDV = GLA_DV // GLA_HEADS
GLA_RANK = 16
GLA_NORMALIZER = 16.0
GLA_CHUNK = 64

RET_HEADS = 4
RET_DK = BRANCH_W
RET_DV = BRANCH_W
RET_HDK = RET_DK // RET_HEADS
RET_HDV = RET_DV // RET_HEADS
RET_CHUNK = 128
ROPE_BASE = 10000.0

IN_SIZES = (3 * HY_W, HY_W,
            SSD_XBC, 2 * SSD_HEADS, SSD_W,
            GLA_DK, GLA_DK, GLA_DV, 2 * GLA_RANK, GLA_DV,
            RET_DK, RET_DK, RET_DV, RET_DV)
N_IN = sum(IN_SIZES)

kernel_name = 'hybrid_bidir_hyena_ssd_gla_retnet'


def layer_norm(x):
    x32 = x.astype(jnp.float32)
    xc = x32 - jnp.mean(x32, axis=-1, keepdims=True)
    return xc * lax.rsqrt(jnp.mean(xc * xc, axis=-1, keepdims=True) + EPS)


def rms_norm(x):
    x32 = x.astype(jnp.float32)
    return x32 * lax.rsqrt(jnp.mean(x32 * x32, axis=-1, keepdims=True) + EPS)


def split_cols(p, sizes):
    offsets = np.cumsum(np.array(sizes))[:-1]
    return jnp.split(p, [int(o) for o in offsets], axis=-1)


def short_conv(u, w, b):
    y = lax.conv_general_dilated(u, w[:, None, :].astype(u.dtype), window_strides=(1,),
                                 padding=[(SHORT_K // 2, SHORT_K // 2)],
                                 dimension_numbers=('NWC', 'WIO', 'NWC'),
                                 feature_group_count=u.shape[-1])
    return y + b.astype(u.dtype)


def chunk_scan(local, decay, s0):
    if s0 is None:
        s0 = jnp.zeros_like(local[:, 0])

    def step(s, inp):
        l, d = inp
        return d * s + l, s

    final, starts = lax.scan(step, s0, (jnp.moveaxis(local, 1, 0), jnp.moveaxis(decay, 1, 0)))
    return jnp.moveaxis(starts, 0, 1), final


def bidirectional(run, ctx_ins, lat_ins, params):
    y_ctx, y_lat = [], []
    for d in range(2):
        f = (lambda a: jnp.flip(a, axis=1)) if d == 1 else (lambda a: a)
        yc, s_ctx = run(*[f(a) for a in ctx_ins[d]], *params[d], None)
        yl, _ = run(*[f(a) for a in lat_ins[d]], *params[d], s_ctx)
        y_ctx.append(f(yc))
        y_lat.append(f(yl))
    return y_ctx[0] + y_ctx[1], y_lat[0] + y_lat[1]


def hyena_filter_spectrum(L, w1, b1, w2, b2, w3, b3, w4, freq):
    f32 = jnp.float32
    w1, b1, w2, b2, w3, b3, w4, freq = [a.astype(f32) for a in (w1, b1, w2, b2, w3, b3, w4, freq)]
    t = jnp.arange(L, dtype=f32)[:, None] / L
    bands = jnp.arange(1, HY_BANDS + 1, dtype=f32)[None, :]
    z = jnp.concatenate([t, jnp.cos(2.0 * math.pi * bands * t), jnp.sin(2.0 * math.pi * bands * t)], axis=-1)
    hdn = jnp.sin(freq[0] * (z @ w1 + b1))
    hdn = jnp.sin(freq[1] * (hdn @ w2 + b2))
    hdn = jnp.sin(freq[2] * (hdn @ w3 + b3))
    filt = (hdn @ w4).reshape(L, HY_ORDER, 2, HY_W)
    deltas = jnp.linspace(math.log(HY_TARGET) / HY_SLOW, math.log(HY_TARGET) / HY_FAST, HY_W)
    window = jnp.exp(-t * jnp.abs(deltas)[None, :])
    filt = filt * window[:, None, None, :]
    fwd = filt[:, :, 0]
    bwd = filt[1:, :, 1]
    circ = jnp.concatenate([fwd, jnp.zeros((1, HY_ORDER, HY_W), f32), jnp.flip(bwd, axis=0)], axis=0)
    return jnp.fft.rfft(circ, n=2 * L, axis=0)


def fft_long_conv(u, spec):
    L = u.shape[1]
    uf = jnp.fft.rfft(u, n=2 * L, axis=1)
    return jnp.fft.irfft(uf * spec[None], n=2 * L, axis=1)[:, :L]


def hyena_seq(u_in, gate, conv_w, conv_b, filt_params, skip):
    L = u_in.shape[1]
    u = short_conv(u_in, conv_w, conv_b).astype(jnp.float32)
    parts = jnp.split(u, HY_ORDER + 1, axis=-1)
    spec = hyena_filter_spectrum(L, *filt_params)
    skip = skip.astype(jnp.float32)
    z = parts[0]
    for o in range(HY_ORDER):
        z = parts[o + 1] * (fft_long_conv(z, spec[:, o]) + z * skip[o])
    return z * jax.nn.silu(gate.astype(jnp.float32))


def ssd_scan(x, bm, cm, dt, a_log, s0):
    b, L = x.shape[:2]
    q = SSD_CHUNK
    nc = L // q
    x = x.reshape(b, nc, q, SSD_GROUPS, SSD_RPG, SSD_HEADDIM)
    bm = bm.reshape(b, nc, q, SSD_GROUPS, SSD_STATE)
    cm = cm.reshape(b, nc, q, SSD_GROUPS, SSD_STATE)
    dt = dt.reshape(b, nc, q, SSD_GROUPS, SSD_RPG)
    a_cum = jnp.cumsum(dt * (-jnp.exp(a_log)), axis=2)
    xdt = x * dt[..., None]
    tri = jnp.tril(jnp.ones((q, q), dtype=bool))[None, None, :, :, None, None]
    seg = a_cum[:, :, :, None] - a_cum[:, :, None, :]
    decay_ij = jnp.exp(jnp.where(tri, seg, -jnp.inf))
    scores = jnp.einsum('bcign,bcjgn->bcijg', cm, bm)
    y = jnp.einsum('bcijgr,bcjgrp->bcigrp', scores[..., None] * decay_ij, xdt)
    a_last = a_cum[:, :, -1:]
    local = jnp.einsum('bcjgn,bcjgrp->bcgrpn', bm, xdt * jnp.exp(a_last - a_cum)[..., None])
    starts, final = chunk_scan(local, jnp.exp(a_last[:, :, 0])[..., None, None], s0)
    y = y + jnp.einsum('bcign,bcgrpn->bcigrp', cm, starts) * jnp.exp(a_cum)[..., None]
    return y.reshape(b, L, SSD_GROUPS, SSD_RPG, SSD_HEADDIM), final


def ssd_prep(xbc, dt_raw, conv_w, conv_b, dt_bias):
    b, L = xbc.shape[:2]
    xbc = jax.nn.silu(short_conv(xbc, conv_w, conv_b).astype(jnp.float32))
    xs, bm, cm = split_cols(xbc, (SSD_W, SSD_GROUPS * SSD_STATE, SSD_GROUPS * SSD_STATE))
    xs = xs.reshape(b, L, SSD_GROUPS, SSD_RPG, SSD_HEADDIM)
    bm = bm.reshape(b, L, SSD_GROUPS, SSD_STATE)
    cm = cm.reshape(b, L, SSD_GROUPS, SSD_STATE)
    dt = jax.nn.softplus(dt_raw.astype(jnp.float32).reshape(b, L, 2, SSD_HEADS) + dt_bias)
    return xs, bm, cm, dt.reshape(b, L, 2, SSD_GROUPS, SSD_RPG)


def ssd_mixer(parts_c, parts_l, conv_w, conv_b, a_log, dt_bias, d_skip, norm_w):
    dt_bias = dt_bias.astype(jnp.float32)
    pc = ssd_prep(parts_c[0], parts_c[1], conv_w, conv_b, dt_bias)
    pl = ssd_prep(parts_l[0], parts_l[1], conv_w, conv_b, dt_bias)
    a_log = a_log.astype(jnp.float32).reshape(2, SSD_GROUPS, SSD_RPG)

    def dir_in(p, d):
        return (p[0], p[1], p[2], p[3][:, :, d])

    yc, yl = bidirectional(ssd_scan, (dir_in(pc, 0), dir_in(pc, 1)), (dir_in(pl, 0), dir_in(pl, 1)),
                           ((a_log[0],), (a_log[1],)))
    d = d_skip.astype(jnp.float32).reshape(SSD_GROUPS, SSD_RPG, 1)

    def finish(y, xs, z):
        b, L = y.shape[:2]
        y = (y + d * xs).reshape(b, L, SSD_W) * jax.nn.silu(z.astype(jnp.float32))
        y = rms_norm(y.reshape(b, L, SSD_GROUPS, SSD_W // SSD_GROUPS)).reshape(b, L, SSD_W)
        return y * norm_w.astype(jnp.float32)

    return finish(yc, pc[0], parts_c[2]), finish(yl, pl[0], parts_l[2])


def gla_scan(q, k, v, g, s0):
    b, L, h, dk = q.shape
    dv = v.shape[-1]
    cs = GLA_CHUNK
    nc = L // cs
    q = q.reshape(b, nc, cs, h, dk)
    k = k.reshape(b, nc, cs, h, dk)
    v = v.reshape(b, nc, cs, h, dv)
    gc = jnp.cumsum(g.reshape(b, nc, cs, h, dk), axis=2)
    g_mid = gc[:, :, cs // 2:cs // 2 + 1]
    g_last = gc[:, :, -1:]
    att = jnp.einsum('bcihk,bcjhk->bchij', q * jnp.exp(gc - g_mid), k * jnp.exp(g_mid - gc))
    att = jnp.where(jnp.tril(jnp.ones((cs, cs), dtype=bool)), att, 0.0)
    y = jnp.einsum('bchij,bcjhv->bcihv', att, v)
    local = jnp.einsum('bcjhk,bcjhv->bchkv', k * jnp.exp(g_last - gc), v)
    starts, final = chunk_scan(local, jnp.exp(g_last[:, :, 0])[..., None], s0)
    y = y + jnp.einsum('bcihk,bchkv->bcihv', q * jnp.exp(gc), starts)
    return y.reshape(b, L, h, dv), final


def gla_mixer(parts_c, parts_l, w2, b2, norm_w):
    f32 = jnp.float32

    def prep(p):
        q, k, v, lr, _ = p
        b, L = q.shape[:2]
        q = q.astype(f32).reshape(b, L, GLA_HEADS, GLA_HDK) * GLA_HDK ** -0.5
        k = k.astype(f32).reshape(b, L, GLA_HEADS, GLA_HDK)
        v = v.astype(f32).reshape(b, L, GLA_HEADS, GLA_HDV)
        lr = lr.astype(f32)
        gs = []
        for d in range(2):
            logit = lr[..., d * GLA_RANK:(d + 1) * GLA_RANK] @ w2[d].astype(f32) + b2[d].astype(f32)
            gs.append((jax.nn.log_sigmoid(logit) / GLA_NORMALIZER).reshape(b, L, GLA_HEADS, GLA_HDK))
        return (q, k, v, gs[0]), (q, k, v, gs[1])

    yc, yl = bidirectional(gla_scan, prep(parts_c), prep(parts_l), ((), ()))

    def finish(y, g):
        b, L = y.shape[:2]
        y = (rms_norm(y) * norm_w.astype(f32)).reshape(b, L, GLA_DV)
        return y * jax.nn.silu(g.astype(f32))

    return finish(yc, parts_c[4]), finish(yl, parts_l[4])


def rope_1d(x, pos):
    d = x.shape[-1]
    inv = ROPE_BASE ** (-jnp.arange(0, d, 2, dtype=jnp.float32) / d)
    ang = pos.astype(jnp.float32)[:, None] * inv[None, :]
    cos, sin = jnp.cos(ang)[:, None, :], jnp.sin(ang)[:, None, :]
    x1, x2 = x[..., :d // 2], x[..., d // 2:]
    return jnp.concatenate([x1 * cos - x2 * sin, x2 * cos + x1 * sin], axis=-1)


def rope_2d(x):
    L = x.shape[1]
    rows = L // GRID_W
    row = jnp.broadcast_to(jnp.arange(rows)[:, None], (rows, GRID_W)).reshape(L)
    col = jnp.broadcast_to(jnp.arange(GRID_W)[None, :], (rows, GRID_W)).reshape(L)
    half = x.shape[-1] // 2
    return jnp.concatenate([rope_1d(x[..., :half], row), rope_1d(x[..., half:], col)], axis=-1)


def retention_scan(q, k, v, lam, s0):
    b, L, h, dk = q.shape
    dv = v.shape[-1]
    cs = RET_CHUNK
    nc = L // cs
    q = q.reshape(b, nc, cs, h, dk)
    k = k.reshape(b, nc, cs, h, dk)
    v = v.reshape(b, nc, cs, h, dv)
    pos = jnp.arange(cs, dtype=jnp.float32)
    lag = pos[:, None] - pos[None, :]
    dmat = jnp.where(lag >= 0, jnp.exp(lam[:, None, None] * jnp.maximum(lag, 0.0)), 0.0)
    att = jnp.einsum('bcihk,bcjhk->bchij', q, k) * dmat
    y = jnp.einsum('bchij,bcjhv->bcihv', att, v)
    to_end = jnp.exp(lam[None, :] * (cs - 1.0 - pos)[:, None])
    local = jnp.einsum('bcjhk,bcjhv->bchkv', k * to_end[:, :, None], v)
    decay = jnp.broadcast_to(jnp.exp(lam * cs)[None, None, :, None, None], (1, nc, h, 1, 1))
    starts, final = chunk_scan(local, decay, s0)
    from_start = jnp.exp(lam[None, :] * (pos + 1.0)[:, None])
    y = y + jnp.einsum('bcihk,bchkv->bcihv', q, starts) * from_start[:, :, None]
    return y.reshape(b, L, h, dv), final


def ret_mixer(parts_c, parts_l, decay_raw):
    f32 = jnp.float32
    lam = -jnp.exp(decay_raw.astype(f32))

    def prep(p, use_rope):
        q, k, v, _ = p
        b, L = q.shape[:2]
        q = q.astype(f32).reshape(b, L, RET_HEADS, RET_HDK)
        k = k.astype(f32).reshape(b, L, RET_HEADS, RET_HDK) * RET_HDK ** -0.5
        v = v.astype(f32).reshape(b, L, RET_HEADS, RET_HDV)
        if use_rope:
            q, k = rope_2d(q), rope_2d(k)
        return (q, k, v)

    ic = prep(parts_c, False)
    il = prep(parts_l, True)
    yc, yl = bidirectional(retention_scan, (ic, ic), (il, il), ((lam[0],), (lam[1],)))

    def finish(y, g):
        b, L = y.shape[:2]
        return layer_norm(y).reshape(b, L, RET_DV) * jax.nn.silu(g.astype(f32))

    return finish(yc, parts_c[3]), finish(yl, parts_l[3])


def modulate_project(s, mod, w_in):
    shift, scale, _ = mod
    h = (layer_norm(s) * (1.0 + scale) + shift).astype(s.dtype)
    return h, split_cols(h @ w_in, IN_SIZES)


def post_residual(s, h, ys, gate, lp):
    m = None
    for i, y in enumerate(ys):
        term = jax.nn.sigmoid(h @ lp['w_gate'][i]) * (y.astype(h.dtype) @ lp['w_br'][i])
        m = term if m is None else m + term
    out = m @ lp['w_out']
    y = layer_norm(ALPHA * s + gate * out) * lp['ln_g'] + lp['ln_b']
    return y.astype(s.dtype)


def trunk_layer(x, ctx, mod_x, mod_c, lp):
    hx, px = modulate_project(x, mod_x, lp['w_in'])
    hc, pc = modulate_project(ctx, mod_c, lp['w_in'])
    filt = (lp['hy_w1'], lp['hy_b1'], lp['hy_w2'], lp['hy_b2'], lp['hy_w3'], lp['hy_b3'], lp['hy_w4'], lp['hy_freq'])
    ys_c = [hyena_seq(pc[0], pc[1], lp['hy_conv_w'], lp['hy_conv_b'], filt, lp['hy_skip'])]
    ys_x = [hyena_seq(px[0], px[1], lp['hy_conv_w'], lp['hy_conv_b'], filt, lp['hy_skip'])]
    yc, yx = ssd_mixer(pc[2:5], px[2:5], lp['ssd_conv_w'], lp['ssd_conv_b'], lp['ssd_a_log'],
                       lp['ssd_dt_bias'], lp['ssd_d'], lp['ssd_norm_w'])
    ys_c.append(yc)
    ys_x.append(yx)
    yc, yx = gla_mixer(pc[5:10], px[5:10], lp['gla_w2'], lp['gla_b2'], lp['gla_norm_w'])
    ys_c.append(yc)
    ys_x.append(yx)
    yc, yx = ret_mixer(pc[10:14], px[10:14], lp['ret_decay'])
    ys_c.append(yc)
    ys_x.append(yx)
    x_new = post_residual(x, hx, ys_x, mod_x[2], lp)
    ctx_new = post_residual(ctx, hc, ys_c, mod_c[2], lp)
    return x_new, ctx_new


def setup_inputs(seed: int = 0) -> dict:
    key = jax.random.key(seed)
    ks = jax.random.split(key, 40)
    f32 = jnp.float32

    def nrm(i, shape, scale):
        return jax.random.normal(ks[i], shape, f32) * scale

    D = D_MODEL
    dt0 = jnp.exp(jax.random.uniform(ks[21], (DEPTH, 2, SSD_HEADS), f32) * (math.log(1e-1) - math.log(1e-3)) + math.log(1e-3))
    ret_base = jnp.log(-jnp.log1p(-(2.0 ** (-5.0 - jnp.arange(RET_HEADS, dtype=f32)))))
    return {
        'x': nrm(0, (BATCH, SEQ, D), 1.0),
        'c': nrm(1, (BATCH, D), 1.0),
        'ctx': nrm(2, (BATCH, CTX_LEN, D), 1.0),
        'c_ctx': nrm(3, (D,), 1.0),
        'w_ada': nrm(4, (DEPTH, D, 3 * D), 0.5 * D ** -0.5),
        'b_ada': nrm(5, (DEPTH, 3 * D), 0.01),
        'w_in': nrm(6, (DEPTH, D, N_IN), D ** -0.5),
        'hy_conv_w': nrm(7, (DEPTH, SHORT_K, 3 * HY_W), SHORT_K ** -0.5),
        'hy_conv_b': nrm(8, (DEPTH, 3 * HY_W), 0.01),
        'hy_w1': nrm(9, (DEPTH, HY_EMB, HY_HID), HY_EMB ** -0.5),
        'hy_b1': nrm(10, (DEPTH, HY_HID), 0.02),
        'hy_w2': nrm(11, (DEPTH, HY_HID, HY_HID), HY_HID ** -0.5),
        'hy_b2': nrm(12, (DEPTH, HY_HID), 0.02),
        'hy_w3': nrm(13, (DEPTH, HY_HID, HY_HID), HY_HID ** -0.5),
        'hy_b3': nrm(14, (DEPTH, HY_HID), 0.02),
        'hy_w4': nrm(15, (DEPTH, HY_HID, HY_ORDER * 2 * HY_W), 0.1 * HY_HID ** -0.5),
        'hy_freq': 1.0 + nrm(16, (DEPTH, 3, HY_HID), 0.01),
        'hy_skip': nrm(17, (DEPTH, HY_ORDER, HY_W), 0.5),
        'ssd_conv_w': nrm(18, (DEPTH, SHORT_K, SSD_XBC), SHORT_K ** -0.5),
        'ssd_conv_b': nrm(19, (DEPTH, SSD_XBC), 0.01),
        'ssd_a_log': jnp.log(jax.random.uniform(ks[20], (DEPTH, 2, SSD_HEADS), f32, 1.0, 16.0)),
        'ssd_dt_bias': dt0 + jnp.log(-jnp.expm1(-dt0)),
        'ssd_d': 1.0 + nrm(22, (DEPTH, SSD_HEADS), 0.01),
        'ssd_norm_w': 1.0 + nrm(23, (DEPTH, SSD_W), 0.01),
        'gla_w2': nrm(24, (DEPTH, 2, GLA_RANK, GLA_DK), GLA_RANK ** -0.5),
        'gla_b2': nrm(25, (DEPTH, 2, GLA_DK), 0.01),
        'gla_norm_w': 1.0 + nrm(26, (DEPTH, GLA_HDV), 0.01),
        'ret_decay': ret_base[None, None, :] + nrm(27, (DEPTH, 2, RET_HEADS), 0.01),
        'w_gate': nrm(28, (DEPTH, N_BRANCH, D, D), D ** -0.5),
        'w_br': nrm(29, (DEPTH, N_BRANCH, BRANCH_W, D), BETA * BRANCH_W ** -0.5),
        'w_out': nrm(30, (DEPTH, D, D), BETA * D ** -0.5),
        'ln_g': 1.0 + nrm(31, (DEPTH, D), 0.01),
        'ln_b': nrm(32, (DEPTH, D), 0.01),
    }


def reference(x, c, ctx, c_ctx, w_ada, b_ada, w_in, hy_conv_w, hy_conv_b, hy_w1, hy_b1, hy_w2, hy_b2,
              hy_w3, hy_b3, hy_w4, hy_freq, hy_skip, ssd_conv_w, ssd_conv_b, ssd_a_log, ssd_dt_bias, ssd_d,
              ssd_norm_w, gla_w2, gla_b2, gla_norm_w, ret_decay, w_gate, w_br, w_out, ln_g, ln_b):
    for l in range(DEPTH):
        mx = jax.nn.silu(c) @ w_ada[l] + b_ada[l]
        mc = jax.nn.silu(c_ctx) @ w_ada[l] + b_ada[l]
        mod_x = [m[:, None, :] for m in jnp.split(mx, 3, axis=-1)]
        mod_c = [m[None, None, :] for m in jnp.split(mc, 3, axis=-1)]
        lp = {
            'w_in': w_in[l], 'hy_conv_w': hy_conv_w[l], 'hy_conv_b': hy_conv_b[l],
            'hy_w1': hy_w1[l], 'hy_b1': hy_b1[l], 'hy_w2': hy_w2[l], 'hy_b2': hy_b2[l],
            'hy_w3': hy_w3[l], 'hy_b3': hy_b3[l], 'hy_w4': hy_w4[l], 'hy_freq': hy_freq[l],
            'hy_skip': hy_skip[l], 'ssd_conv_w': ssd_conv_w[l], 'ssd_conv_b': ssd_conv_b[l],
            'ssd_a_log': ssd_a_log[l], 'ssd_dt_bias': ssd_dt_bias[l], 'ssd_d': ssd_d[l],
            'ssd_norm_w': ssd_norm_w[l], 'gla_w2': gla_w2[l], 'gla_b2': gla_b2[l],
            'gla_norm_w': gla_norm_w[l], 'ret_decay': ret_decay[l], 'w_gate': w_gate[l],
            'w_br': w_br[l], 'w_out': w_out[l], 'ln_g': ln_g[l], 'ln_b': ln_b[l],
        }
        x, ctx = trunk_layer(x, ctx, mod_x, mod_c, lp)
    return x
```

```python
import functools
import math

import jax
import jax.numpy as jnp
from jax import lax
from jax.experimental import pallas as pl
from jax.experimental.pallas import tpu as pltpu

F32 = jnp.float32
BF16 = jnp.bfloat16
HI = lax.Precision.HIGHEST

D_MODEL = 4096
DEPTH = 2
BRANCH_W = 1024
GRID_W = 64
EPS = 1e-6
ALPHA = (2 * DEPTH) ** 0.25

HY_BANDS = 16
HY_HID = 64
HY_TARGET = 1e-2
HY_FAST = 0.3
HY_SLOW = 1.5

SSD_HEADS = 16
SSD_HEADDIM = 64
SSD_GROUPS = 4
SSD_RPG = 4
SSD_STATE = 128
SSD_CHUNK = 128

GLA_HEADS = 4
GLA_HDK = 128
GLA_HDV = 256
GLA_RANK = 16
GLA_NORMALIZER = 16.0
GLA_CHUNK = 64

RET_HEADS = 4
RET_HD = 256
RET_CHUNK = 128
ROPE_BASE = 10000.0

NP = 14336
C_HYIN, C_HYGATE, C_XBC, C_SSDZ = 0, 3072, 4096, 6144
C_GQ, C_GK, C_GV, C_GG = 7168, 7680, 8192, 9216
C_RQ, C_RK, C_RV, C_RG = 10240, 11264, 12288, 13312
NSMALL = 128
LANE = 128
DFT_MINOR = 128

VMEM_LIMIT = 52 * 1024 * 1024


def _cp(*sem):
    return pltpu.CompilerParams(dimension_semantics=sem, vmem_limit_bytes=VMEM_LIMIT)


def _pick(n, cands):
    for c in cands:
        if n % c == 0:
            return c
    raise ValueError(f"no tile for {n} in {cands}")


def _silu(x):
    return x * jax.nn.sigmoid(x)


def _softplus(x):
    return jnp.maximum(x, 0.0) + jnp.log1p(jnp.exp(-jnp.abs(x)))


def _log_sigmoid(x):
    return jnp.minimum(x, 0.0) - jnp.log1p(jnp.exp(-jnp.abs(x)))


def _nt(a, b):
    return lax.dot_general(a, b, (((1,), (1,)), ((), ())), preferred_element_type=F32)


def _tn(a, b):
    return lax.dot_general(a, b, (((0,), (0,)), ((), ())), preferred_element_type=F32)


def _dot(a, b):
    return jnp.dot(a, b, preferred_element_type=F32)


def _dot_hi(a, b):
    return jnp.dot(a, b, preferred_element_type=F32, precision=HI)


def _ada_kernel(c_ref, w_ref, b_ref, o_ref):
    a = _silu(c_ref[...]).astype(BF16)
    o_ref[...] = _dot(a, w_ref[...].astype(BF16)) + b_ref[...]


def ada_modulation(cs, w_ada, b_ada):
    depth, d, n = w_ada.shape
    tn = 512
    return pl.pallas_call(
        _ada_kernel,
        grid=(depth, n // tn),
        in_specs=[pl.BlockSpec((8, d), lambda l, j: (0, 0)),
                  pl.BlockSpec((None, d, tn), lambda l, j: (l, 0, j)),
                  pl.BlockSpec((None, 1, tn), lambda l, j: (l, 0, j))],
        out_specs=pl.BlockSpec((None, 8, tn), lambda l, j: (l, 0, j)),
        out_shape=jax.ShapeDtypeStruct((depth, 8, n), F32),
        compiler_params=_cp("arbitrary", "arbitrary"),
        name="ada_modulation",
    )(cs, w_ada, b_ada.reshape(depth, 1, n))


def _ln_rows(x):
    xc = x - jnp.mean(x, axis=-1, keepdims=True)
    return xc * lax.rsqrt(jnp.mean(xc * xc, axis=-1, keepdims=True) + EPS)


def _lnmod_kernel(*refs, pre_ln, emit_h, d):
    it = iter(refs)
    s_ref = next(it)
    mod_ref = next(it) if emit_h else None
    g_ref = next(it) if pre_ln else None
    b_ref = next(it) if pre_ln else None
    s_out = next(it) if pre_ln else None
    h_out = next(it) if emit_h else None
    x = s_ref[...]
    if pre_ln:
        x = _ln_rows(x) * g_ref[...] + b_ref[...]
        s_out[...] = x
    if emit_h:
        shift = mod_ref[:, 0:d]
        scale = mod_ref[:, d:2 * d]
        h_out[...] = (_ln_rows(x) * (1.0 + scale) + shift).astype(BF16)


def ln_modulate(s, mod_rows, ln_g, ln_b, n_lat_rows, *, pre_ln, emit_h):
    b, t, d = s.shape
    tr = 256
    nlat = n_lat_rows // tr
    args = [s]
    in_specs = [pl.BlockSpec((None, tr, d), lambda bi, ti: (bi, ti, 0))]
    if emit_h:
        args.append(mod_rows)
        in_specs.append(pl.BlockSpec((None, 1, 3 * d), lambda bi, ti: (jnp.where(ti < nlat, bi, 2), 0, 0)))
    if pre_ln:
        args += [ln_g.reshape(1, d), ln_b.reshape(1, d)]
        in_specs += [pl.BlockSpec((1, d), lambda bi, ti: (0, 0))] * 2
    out_shape, out_specs = [], []
    if pre_ln:
        out_shape.append(jax.ShapeDtypeStruct((b, t, d), F32))
        out_specs.append(pl.BlockSpec((None, tr, d), lambda bi, ti: (bi, ti, 0)))
    if emit_h:
        out_shape.append(jax.ShapeDtypeStruct((b, t, d), BF16))
        out_specs.append(pl.BlockSpec((None, tr, d), lambda bi, ti: (bi, ti, 0)))
    return pl.pallas_call(
        functools.partial(_lnmod_kernel, pre_ln=pre_ln, emit_h=emit_h, d=d),
        grid=(b, t // tr), in_specs=in_specs, out_specs=out_specs, out_shape=out_shape,
        compiler_params=_cp("arbitrary", "arbitrary"), name="ln_modulate",
    )(*args)


def _mm_kernel(a_ref, w_ref, o_ref):
    o_ref[...] = _dot(a_ref[...], w_ref[...]).astype(o_ref.dtype)


def matmul(a, w, out_dtype, name):
    m, k = a.shape
    n = w.shape[1]
    tm = _pick(m, (768, 512, 256))
    tn = _pick(n, (1024, 512, 128))
    return pl.pallas_call(
        _mm_kernel, grid=(m // tm, n // tn),
        in_specs=[pl.BlockSpec((tm, k), lambda i, j: (i, 0)),
                  pl.BlockSpec((k, tn), lambda i, j: (0, j))],
        out_specs=pl.BlockSpec((tm, tn), lambda i, j: (i, j)),
        out_shape=jax.ShapeDtypeStruct((m, n), out_dtype),
        compiler_params=_cp("arbitrary", "arbitrary"), name=name,
    )(a, w)


def _merge_kernel(h_ref, y_ref, wg_ref, wb_ref, o_ref, acc_ref):
    i = pl.program_id(2)
    term = jax.nn.sigmoid(_dot(h_ref[...], wg_ref[...])) * _dot(y_ref[...], wb_ref[...])

    @pl.when(i == 0)
    def _():
        acc_ref[...] = term

    @pl.when(i > 0)
    def _():
        acc_ref[...] += term

    @pl.when(i == pl.num_programs(2) - 1)
    def _():
        o_ref[...] = acc_ref[...].astype(o_ref.dtype)


def gated_merge(h, y, w_gate, w_br):
    m, d = h.shape
    nb, bw, _ = w_br.shape
    tm = _pick(m, (768, 512, 256))
    tn = 512
    return pl.pallas_call(
        _merge_kernel, grid=(m // tm, d // tn, nb),
        in_specs=[pl.BlockSpec((tm, d), lambda i, j, r: (i, 0)),
                  pl.BlockSpec((tm, bw), lambda i, j, r: (i, r)),
                  pl.BlockSpec((None, d, tn), lambda i, j, r: (r, 0, j)),
                  pl.BlockSpec((None, bw, tn), lambda i, j, r: (r, 0, j))],
        out_specs=pl.BlockSpec((tm, tn), lambda i, j, r: (i, j)),
        out_shape=jax.ShapeDtypeStruct((m, d), BF16),
        scratch_shapes=[pltpu.VMEM((tm, tn), F32)],
        compiler_params=_cp("arbitrary", "arbitrary", "arbitrary"), name="gated_merge",
    )(h, y, w_gate, w_br)


def _outproj_kernel(m_ref, w_ref, s_ref, gl_ref, gc_ref, o_ref, *, tm, rows_per_batch, n_lat_rows):
    out = _dot(m_ref[...], w_ref[...])
    row = pl.program_id(0) * tm + lax.broadcasted_iota(jnp.int32, (tm, 1), 0)
    is_lat = (row % rows_per_batch) < n_lat_rows
    gate = jnp.where(is_lat, gl_ref[...], gc_ref[...])
    o_ref[...] = ALPHA * s_ref[...] + gate * out


def out_projection(m2, w_out, s2, gate_rows, rows_per_batch, n_lat_rows):
    m, d = m2.shape
    tm = _pick(rows_per_batch, (768, 512, 256))
    tn = 1024
    bpb = rows_per_batch // tm
    return pl.pallas_call(
        functools.partial(_outproj_kernel, tm=tm, rows_per_batch=rows_per_batch, n_lat_rows=n_lat_rows),
        grid=(m // tm, d // tn),
        in_specs=[pl.BlockSpec((tm, d), lambda i, j: (i, 0)),
                  pl.BlockSpec((d, tn), lambda i, j: (0, j)),
                  pl.BlockSpec((tm, tn), lambda i, j: (i, j)),
                  pl.BlockSpec((None, 1, tn), lambda i, j: (i // bpb, 0, j)),
                  pl.BlockSpec((None, 1, tn), lambda i, j: (2, 0, j))],
        out_specs=pl.BlockSpec((tm, tn), lambda i, j: (i, j)),
        out_shape=jax.ShapeDtypeStruct((m, d), F32),
        compiler_params=_cp("arbitrary", "arbitrary"), name="out_projection",
    )(m2, w_out, s2, gate_rows, gate_rows)


CONV_ROWS = 256


def _conv_kernel(x_ref, w_ref, b_ref, o_ref, *, t_rows, n_lat_rows, act):
    r = CONV_ROWS
    w0, w1, w2 = w_ref[0:1, :], w_ref[1:2, :], w_ref[2:3, :]
    bias = b_ref[...]
    rid = lax.broadcasted_iota(jnp.int32, (r, 1), 0)

    def body(i, carry):
        r0 = pl.multiple_of(i * r, r)
        cur = x_ref[pl.ds(r0, r), :].astype(F32)
        p0 = pl.multiple_of(jnp.maximum(r0 - 8, 0), 8)
        n0 = pl.multiple_of(jnp.minimum(r0 + r, t_rows - 8), 8)
        prev_row = x_ref[pl.ds(p0, 8), :].astype(F32)[7:8, :]
        next_row = x_ref[pl.ds(n0, 8), :].astype(F32)[0:1, :]
        up = jnp.where(rid == 0, prev_row, pltpu.roll(cur, 1, 0))
        dn = jnp.where(rid == r - 1, next_row, pltpu.roll(cur, r - 1, 0))
        gpos = r0 + rid
        up = jnp.where((gpos == 0) | (gpos == n_lat_rows), 0.0, up)
        dn = jnp.where((gpos == n_lat_rows - 1) | (gpos == t_rows - 1), 0.0, dn)
        y = w0 * up + w1 * cur + w2 * dn + bias
        if act:
            y = _silu(y)
        o_ref[pl.ds(r0, r), :] = y.astype(o_ref.dtype)
        return carry

    lax.fori_loop(0, t_rows // r, body, 0)


def short_conv(p, col0, width, w, b, n_lat_rows, act, out_dtype, name):
    bsz, t, _ = p.shape
    ct = 256
    cb = col0 // ct
    return pl.pallas_call(
        functools.partial(_conv_kernel, t_rows=t, n_lat_rows=n_lat_rows, act=act),
        grid=(bsz, width // ct),
        in_specs=[pl.BlockSpec((None, t, ct), lambda bi, j: (bi, 0, cb + j)),
                  pl.BlockSpec((3, ct), lambda bi, j: (0, j)),
                  pl.BlockSpec((1, ct), lambda bi, j: (0, j))],
        out_specs=pl.BlockSpec((None, t, ct), lambda bi, j: (bi, 0, j)),
        out_shape=jax.ShapeDtypeStruct((bsz, t, width), out_dtype),
        compiler_params=_cp("arbitrary", "arbitrary"), name=name,
    )(p, w, b.reshape(1, width))


def _filter_kernel(w1_ref, b1_ref, w2_ref, b2_ref, w3_ref, b3_ref, fr_ref, w4_ref, dl_ref, o_ref, *, seq, tr):
    n = pl.program_id(0) * tr + lax.broadcasted_iota(jnp.int32, (tr, 1), 0)
    lag = jnp.where(n < seq, n, 2 * seq - n).astype(F32)
    t = lag * (1.0 / seq)
    lane = lax.broadcasted_iota(jnp.int32, (1, LANE), 1)
    band = jnp.where(lane <= HY_BANDS, lane, lane - HY_BANDS).astype(F32)
    arg = (jnp.float32(2.0 * math.pi) * band) * t
    z = jnp.where(lane == 0, t,
                  jnp.where(lane <= HY_BANDS, jnp.cos(arg),
                            jnp.where(lane <= 2 * HY_BANDS, jnp.sin(arg), 0.0)))
    hdn = jnp.sin(fr_ref[0:1, :] * (_dot_hi(z, w1_ref[...]) + b1_ref[...]))
    hdn = jnp.sin(fr_ref[1:2, :] * (_dot_hi(hdn, w2_ref[...]) + b2_ref[...]))
    hdn = jnp.sin(fr_ref[2:3, :] * (_dot_hi(hdn, w3_ref[...]) + b3_ref[...]))
    filt = _dot_hi(hdn, w4_ref[...])
    win = jnp.exp(-t * dl_ref[...])
    win = jnp.where(n == seq, 0.0, win)
    wdt = o_ref.shape[-1]
    o_ref[0] = filt[:, :wdt] * win
    o_ref[1] = filt[:, wdt:] * win


def hyena_filter(seq, w1p, b1, w2, b2, w3, b3, freq, w4d, deltas_abs):
    tr = 256
    wdt = deltas_abs.shape[-1]
    nblk = 2 * seq // tr
    half = seq // tr
    full = lambda shape: pl.BlockSpec(shape, lambda i: (0,) * len(shape))
    return pl.pallas_call(
        functools.partial(_filter_kernel, seq=seq, tr=tr),
        grid=(nblk,),
        in_specs=[full((LANE, HY_HID)), full((1, HY_HID)), full((HY_HID, HY_HID)), full((1, HY_HID)),
                  full((HY_HID, HY_HID)), full((1, HY_HID)), full((3, HY_HID)),
                  pl.BlockSpec((None, HY_HID, 2 * wdt), lambda i: (jnp.where(i < half, 0, 1), 0, 0)),
                  full((1, wdt))],
        out_specs=pl.BlockSpec((2, tr, wdt), lambda i: (0, i, 0)),
        out_shape=jax.ShapeDtypeStruct((2, 2 * seq, wdt), F32),
        compiler_params=_cp("arbitrary"), name="hyena_filter",
    )(w1p, b1, w2, b2, w3, b3, freq, w4d, deltas_abs)


def _dft_tables(n1):
    n = n1 * DFT_MINOR
    two_pi = 2.0 * math.pi
    ia = jnp.arange(n1, dtype=jnp.int32)
    ib = jnp.arange(DFT_MINOR, dtype=jnp.int32)
    m = (ia[None, :, None] * (DFT_MINOR * ia[None, None, :] + ib[:, None, None])) % n
    ang = m.astype(F32) * (two_pi / n)
    g = jnp.concatenate([jnp.cos(ang), -jnp.sin(ang)], axis=1).astype(BF16)
    th = ((ib[:, None] * ib[None, :]) % DFT_MINOR).astype(F32) * (two_pi / DFT_MINOR)
    c, s = jnp.cos(th), jnp.sin(th)
    mf = jnp.concatenate([jnp.concatenate([c, s], 1), jnp.concatenate([-s, c], 1)], 0).astype(BF16)
    kk = ia[:, None, None] + n1 * ib[None, None, :]
    mi = (ib[None, :, None] * kk) % n
    ps = mi.astype(F32) * (two_pi / n)
    cr, ci = jnp.cos(ps), jnp.sin(ps)
    cinv = jnp.concatenate([jnp.concatenate([cr, -ci], 2), jnp.concatenate([ci, cr], 2)], 1).astype(BF16)
    ph = ((ia[: n1 // 2, None] * ia[None, :]) % n1).astype(F32) * (two_pi / n1)
    m3 = (jnp.concatenate([jnp.cos(ph), -jnp.sin(ph)], 1) * (1.0 / n)).astype(BF16)
    return g, mf, cinv, m3


def _s1_kernel(g_ref, x_ref, o_ref):
    o_ref[...] = _dot(g_ref[...], x_ref[...].astype(BF16)).astype(o_ref.dtype)


def dft_stage1(xv, g, lane_blocks_per_b, comp, rows, wdt, name):
    gsz = xv.shape[0]
    n1x2 = g.shape[1]
    return pl.pallas_call(
        _s1_kernel, grid=(gsz, DFT_MINOR),
        in_specs=[pl.BlockSpec((None, n1x2, rows), lambda i, bb: (bb, 0, 0)),
                  pl.BlockSpec((None, rows, wdt), lambda i, bb: (i, 0, bb * lane_blocks_per_b + comp))],
        out_specs=pl.BlockSpec((None, n1x2, wdt), lambda i, bb: (i, 0, bb)),
        out_shape=jax.ShapeDtypeStruct((gsz, n1x2, DFT_MINOR * wdt), BF16),
        compiler_params=_cp("arbitrary", "arbitrary"), name=name,
    )(g, xv)


def _s2f_kernel(mf_ref, a_ref, o_ref):
    wdt = a_ref.shape[-1]
    y = _dot(mf_ref[...], a_ref[...].reshape(2 * DFT_MINOR, wdt))
    o_ref[...] = y.reshape(2, DFT_MINOR, wdt)


def filter_spectrum(a5, mf):
    no, _, n1, _, wdt = a5.shape
    spec = pl.BlockSpec((None, 2, None, DFT_MINOR, wdt), lambda o, c: (o, 0, c, 0, 0))
    return pl.pallas_call(
        _s2f_kernel, grid=(no, n1),
        in_specs=[pl.BlockSpec((2 * DFT_MINOR, 2 * DFT_MINOR), lambda o, c: (0, 0)), spec],
        out_specs=spec,
        out_shape=jax.ShapeDtypeStruct(a5.shape, F32),
        compiler_params=_cp("arbitrary", "arbitrary"), name="hyena_filter_spectrum",
    )(mf, a5)


def _s2_kernel(mf_ref, ci_ref, h_ref, a_ref, o_ref):
    wdt = a_ref.shape[-1]
    y = _dot(mf_ref[...], a_ref[...].reshape(2 * DFT_MINOR, wdt))
    yr, yi = y[:DFT_MINOR], y[DFT_MINOR:]
    hr, hi = h_ref[0], h_ref[1]
    z = jnp.concatenate([yr * hr - yi * hi, yr * hi + yi * hr], axis=0).astype(BF16)
    o_ref[...] = _dot(ci_ref[...], z).reshape(2, DFT_MINOR, wdt).astype(o_ref.dtype)


def spectral_multiply(a5, spec5, order, mf, cinv):
    bsz, _, n1, _, wdt = a5.shape
    blk = pl.BlockSpec((None, 2, None, DFT_MINOR, wdt), lambda c, bi: (bi, 0, c, 0, 0))
    return pl.pallas_call(
        _s2_kernel, grid=(n1, bsz),
        in_specs=[pl.BlockSpec((2 * DFT_MINOR, 2 * DFT_MINOR), lambda c, bi: (0, 0)),
                  pl.BlockSpec((None, 2 * DFT_MINOR, 2 * DFT_MINOR), lambda c, bi: (c, 0, 0)),
                  pl.BlockSpec((None, 2, None, DFT_MINOR, wdt), lambda c, bi: (order, 0, c, 0, 0)),
                  blk],
        out_specs=blk,
        out_shape=jax.ShapeDtypeStruct(a5.shape, BF16),
        compiler_params=_cp("arbitrary", "arbitrary"), name="hyena_spectral_multiply",
    )(mf, cinv, spec5, a5)


def _s3_mid_kernel(m3_ref, g_ref, bq_ref, z_ref, x_ref, sk_ref, zo_ref, ao_ref):
    y = _dot(m3_ref[...], bq_ref[...])
    z = x_ref[...] * (y + z_ref[...] * sk_ref[...])
    zo_ref[...] = z
    ao_ref[...] = _dot(g_ref[...], z.astype(BF16)).astype(ao_ref.dtype)


def _s3_last_kernel(m3_ref, bq_ref, z_ref, x_ref, sk_ref, gate_ref, o_ref):
    y = _dot(m3_ref[...], bq_ref[...])
    z = x_ref[...] * (y + z_ref[...] * sk_ref[...])
    o_ref[...] = (z * _silu(gate_ref[...].astype(F32))).astype(o_ref.dtype)


def _hyena_ctx_kernel(ff_ref, fd_ref, fi_ref, circ_ref, v_ref, x1_ref, x2_ref, gate_ref, sk_ref, o_ref, *, nc):
    z = v_ref[...].astype(F32)
    xs = (x1_ref, x2_ref)
    for o in range(2):
        hs = _dot(ff_ref[...], circ_ref[o].astype(BF16))
        us = _dot(fd_ref[...], z.astype(BF16))
        hr, hi = hs[:nc], hs[nc:]
        ur, ui = us[:nc], us[nc:]
        zz = jnp.concatenate([ur * hr - ui * hi, ur * hi + ui * hr], axis=0).astype(BF16)
        y = _dot(fi_ref[...], zz)
        z = xs[o][...].astype(F32) * (y + z * sk_ref[o:o + 1, :])
    o_ref[...] = (z * _silu(gate_ref[...].astype(F32))).astype(o_ref.dtype)


def hyena_context(hv, p, circ_c, skip, n_lat_rows, lc):
    bsz, _, w3 = hv.shape
    wdt = w3 // 3
    nc = 2 * lc
    ct = 256
    two_pi = 2.0 * math.pi
    ik = jnp.arange(nc, dtype=jnp.int32)
    th = ((ik[:, None] * ik[None, :]) % nc).astype(F32) * (two_pi / nc)
    c, s = jnp.cos(th), jnp.sin(th)
    ffull = jnp.concatenate([c, -s], axis=0).astype(BF16)
    fdata = ffull[:, :lc]
    finv = (jnp.concatenate([c[:lc], -s[:lc]], axis=1) * (1.0 / nc)).astype(BF16)
    rb = n_lat_rows // lc
    cw = wdt // ct
    full = lambda shape: pl.BlockSpec(shape, lambda bi, j: (0,) * len(shape))
    return pl.pallas_call(
        functools.partial(_hyena_ctx_kernel, nc=nc),
        grid=(bsz, cw),
        in_specs=[full((2 * nc, nc)), full((2 * nc, lc)), full((lc, 2 * nc)),
                  pl.BlockSpec((2, nc, ct), lambda bi, j: (0, 0, j)),
                  pl.BlockSpec((None, lc, ct), lambda bi, j: (bi, rb, j)),
                  pl.BlockSpec((None, lc, ct), lambda bi, j: (bi, rb, cw + j)),
                  pl.BlockSpec((None, lc, ct), lambda bi, j: (bi, rb, 2 * cw + j)),
                  pl.BlockSpec((None, lc, ct), lambda bi, j: (bi, rb, C_HYGATE // ct + j)),
                  pl.BlockSpec((2, ct), lambda bi, j: (0, j))],
        out_specs=pl.BlockSpec((None, lc, ct), lambda bi, j: (bi, 0, j)),
        out_shape=jax.ShapeDtypeStruct((bsz, lc, wdt), BF16),
        compiler_params=_cp("arbitrary", "arbitrary"), name="hyena_context",
    )(ffull, fdata, finv, circ_c, hv, hv, hv, p, skip)


def hyena_latent(hv, p, circ_l, skip, n_lat_rows, tables):
    g, mf, cinv, m3 = tables
    bsz, t, w3 = hv.shape
    wdt = w3 // 3
    n1 = 2 * n_lat_rows // DFT_MINOR
    half = n1 // 2
    trow = t // DFT_MINOR
    npw = p.shape[-1] // wdt
    af = dft_stage1(circ_l.reshape(2, n1, DFT_MINOR * wdt), g, 1, 0, n1, wdt, "hyena_filter_stage1")
    spec5 = filter_spectrum(af.reshape(2, 2, n1, DFT_MINOR, wdt), mf)
    gd = g[:, :, :half]
    hv_v = hv.reshape(bsz, trow, DFT_MINOR * w3)
    p_v = p.reshape(bsz, trow, DFT_MINOR * p.shape[-1])
    sk = skip.reshape(2, 1, wdt)
    a = dft_stage1(hv_v, gd, 3, 0, half, wdt, "hyena_stage1")
    bq = spectral_multiply(a.reshape(bsz, 2, n1, DFT_MINOR, wdt), spec5, 0, mf, cinv)
    bq = bq.reshape(bsz, 2 * n1, DFT_MINOR * wdt)
    tile = lambda comp, nlb: pl.BlockSpec((None, half, wdt), lambda bi, bb: (bi, 0, bb * nlb + comp))
    z1, a = pl.pallas_call(
        _s3_mid_kernel, grid=(bsz, DFT_MINOR),
        in_specs=[pl.BlockSpec((half, 2 * n1), lambda bi, bb: (0, 0)),
                  pl.BlockSpec((None, 2 * n1, half), lambda bi, bb: (bb, 0, 0)),
                  pl.BlockSpec((None, 2 * n1, wdt), lambda bi, bb: (bi, 0, bb)),
                  tile(0, 3), tile(1, 3),
                  pl.BlockSpec((None, 1, wdt), lambda bi, bb: (0, 0, 0))],
        out_specs=[tile(0, 1), pl.BlockSpec((None, 2 * n1, wdt), lambda bi, bb: (bi, 0, bb))],
        out_shape=[jax.ShapeDtypeStruct((bsz, half, DFT_MINOR * wdt), F32),
                   jax.ShapeDtypeStruct((bsz, 2 * n1, DFT_MINOR * wdt), BF16)],
        compiler_params=_cp("arbitrary", "arbitrary"), name="hyena_stage3_mid",
    )(m3, gd, bq, hv_v, hv_v, sk)
    bq = spectral_multiply(a.reshape(bsz, 2, n1, DFT_MINOR, wdt), spec5, 1, mf, cinv)
    bq = bq.reshape(bsz, 2 * n1, DFT_MINOR * wdt)
    y = pl.pallas_call(
        _s3_last_kernel, grid=(bsz, DFT_MINOR),
        in_specs=[pl.BlockSpec((half, 2 * n1), lambda bi, bb: (0, 0)),
                  pl.BlockSpec((None, 2 * n1, wdt), lambda bi, bb: (bi, 0, bb)),
                  tile(0, 1), tile(2, 3),
                  pl.BlockSpec((None, 1, wdt), lambda bi, bb: (1, 0, 0)),
                  tile(C_HYGATE // wdt, npw)],
        out_specs=tile(0, 1),
        out_shape=jax.ShapeDtypeStruct((bsz, half, DFT_MINOR * wdt), BF16),
        compiler_params=_cp("arbitrary", "arbitrary"), name="hyena_stage3_last",
    )(m3, bq, z1, hv_v, sk, p_v)
    return y.reshape(bsz, n_lat_rows, wdt)


def _scan_chunk_map(direction, n_lat_chunks, n_chunks):
    if direction == 0:
        return lambda t: (t + n_lat_chunks) % n_chunks
    return lambda t: n_chunks - 1 - t


def _tri_mask(n, direction):
    ri = lax.broadcasted_iota(jnp.int32, (n, n), 0)
    ci = lax.broadcasted_iota(jnp.int32, (n, n), 1)
    return (ci <= ri) if direction == 0 else (ci >= ri)


def _ssd_kernel(*refs, direction):
    d = direction
    if d == 0:
        xs_ref, bm_ref, cm_ref, sm_ref, dtb_ref, alog_ref, e_ref, o_ref, st_ref = refs
    else:
        (xs_ref, bm_ref, cm_ref, sm_ref, dtb_ref, alog_ref, e_ref,
         y0_ref, z_ref, dsk_ref, nw_ref, o_ref, st_ref) = refs
    q = SSD_CHUNK
    hd = SSD_HEADDIM
    gw = SSD_RPG * hd

    @pl.when(pl.program_id(1) == 0)
    def _():
        st_ref[...] = jnp.zeros_like(st_ref)

    xs = xs_ref[...].astype(F32)
    bm = bm_ref[...].astype(BF16)
    cm = cm_ref[...].astype(BF16)
    dt = _softplus(sm_ref[...] + dtb_ref[...])
    a = dt * (-jnp.exp(alog_ref[...]))
    tri = _tri_mask(q, d)
    acum = _dot_hi(tri.astype(F32), a)
    acum_t = acum.T
    expand = e_ref[...]
    dtx = _dot_hi(dt, expand)
    ax = _dot_hi(acum, expand)
    atx = ax[q - 1:q, :] if d == 0 else ax[0:1, :]
    xdt = xs * dtx
    e_in = jnp.exp(ax)
    xw = (xdt * jnp.exp(atx - ax)).astype(BF16)
    xdt_b = xdt.astype(BF16)
    ys = []
    for g in range(SSD_GROUPS):
        bg = bm[:, g * SSD_STATE:(g + 1) * SSD_STATE]
        cg = cm[:, g * SSD_STATE:(g + 1) * SSD_STATE]
        sc = _nt(cg, bg)
        st_g = st_ref[g * gw:(g + 1) * gw, :]
        y_int = _nt(cg, st_g.astype(BF16))
        yh, decs = [], []
        for r in range(SSD_RPG):
            hl = SSD_HEADS * d + SSD_RPG * g + r
            seg = acum[:, hl:hl + 1] - acum_t[hl:hl + 1, :]
            dm = jnp.exp(jnp.where(tri, seg, -1e30))
            ch = (SSD_RPG * g + r) * hd
            yh.append(_dot((sc * dm).astype(BF16), xdt_b[:, ch:ch + hd]))
            a_tot = acum[q - 1:q, hl:hl + 1] if d == 0 else acum[0:1, hl:hl + 1]
            decs.append(jnp.broadcast_to(jnp.exp(a_tot), (hd, SSD_STATE)))
        ys.append(jnp.concatenate(yh, axis=1) + y_int * e_in[:, g * gw:(g + 1) * gw])
        st_ref[g * gw:(g + 1) * gw, :] = st_g * jnp.concatenate(decs, axis=0) + _tn(xw[:, g * gw:(g + 1) * gw], bg)
    y = jnp.concatenate(ys, axis=1)
    if d == 0:
        o_ref[...] = y
    else:
        y = (y0_ref[...] + y + dsk_ref[...] * xs) * _silu(z_ref[...].astype(F32))
        parts = []
        for g in range(SSD_GROUPS):
            yg = y[:, g * gw:(g + 1) * gw]
            parts.append(yg * lax.rsqrt(jnp.mean(yg * yg, axis=-1, keepdims=True) + EPS))
        o_ref[...] = (jnp.concatenate(parts, axis=1) * nw_ref[...]).astype(o_ref.dtype)


def ssd_mixer(xbc, p, psmall, dtb, alog, expand, dskip, normw, n_lat_rows):
    bsz, t, _ = xbc.shape
    q = SSD_CHUNK
    w = BRANCH_W
    nct, ncl = t // q, n_lat_rows // q
    outs = None
    for d in range(2):
        cm_ = _scan_chunk_map(d, ncl, nct)
        row = lambda width, cb, cm_=cm_: pl.BlockSpec((None, q, width), lambda bi, ti: (bi, cm_(ti), cb))
        const = lambda shape: pl.BlockSpec(shape, lambda bi, ti: (0,) * len(shape))
        in_specs = [row(w, 0), row(4 * SSD_STATE, 2), row(4 * SSD_STATE, 3), row(NSMALL, 0),
                    const((1, NSMALL)), const((1, NSMALL)),
                    pl.BlockSpec((None, NSMALL, w), lambda bi, ti, d=d: (d, 0, 0))]
        args = [xbc, xbc, xbc, psmall, dtb, alog, expand]
        if d == 1:
            in_specs += [row(w, 0), row(w, C_SSDZ // w), const((1, w)), const((1, w))]
            args += [outs, p, dskip, normw]
        outs = pl.pallas_call(
            functools.partial(_ssd_kernel, direction=d), grid=(bsz, nct),
            in_specs=in_specs, out_specs=row(w, 0),
            out_shape=jax.ShapeDtypeStruct((bsz, t, w), F32 if d == 0 else BF16),
            scratch_shapes=[pltpu.VMEM((SSD_HEADS * SSD_HEADDIM, SSD_STATE), F32)],
            compiler_params=_cp("arbitrary", "arbitrary"), name=f"ssd_scan_dir{d}",
        )(*args)
    return outs


def _gla_kernel(*refs, direction):
    d = direction
    if d == 0:
        q_ref, k_ref, v_ref, sm_ref, w2_ref, b2_ref, o_ref, st_ref = refs
    else:
        q_ref, k_ref, v_ref, sm_ref, w2_ref, b2_ref, y0_ref, g_ref, nw_ref, o_ref, st_ref = refs
    cs = GLA_CHUNK
    dk, dv = GLA_HDK, GLA_HDV

    @pl.when(pl.program_id(1) == 0)
    def _():
        st_ref[...] = jnp.zeros_like(st_ref)

    logit = _dot(sm_ref[...].astype(BF16), w2_ref[...]) + b2_ref[...]
    gl = _log_sigmoid(logit) * (1.0 / GLA_NORMALIZER)
    tri = _tri_mask(cs, d)
    gc = _dot_hi(tri.astype(F32), gl)
    mid = cs // 2 if d == 0 else cs - 1 - cs // 2
    g_mid = gc[mid:mid + 1, :]
    g_last = gc[cs - 1:cs, :] if d == 0 else gc[0:1, :]
    qf = q_ref[...].astype(F32) * (dk ** -0.5)
    kf = k_ref[...].astype(F32)
    vb = v_ref[...].astype(BF16)
    qa = (qf * jnp.exp(gc - g_mid)).astype(BF16)
    ka = (kf * jnp.exp(g_mid - gc)).astype(BF16)
    qs = (qf * jnp.exp(gc)).astype(BF16)
    ke = (kf * jnp.exp(g_last - gc)).astype(BF16)
    dec = jnp.exp(g_last)
    ys = []
    for h in range(GLA_HEADS):
        ks, vs = slice(h * dk, (h + 1) * dk), slice(h * dv, (h + 1) * dv)
        att = jnp.where(tri, _nt(qa[:, ks], ka[:, ks]), 0.0)
        st_h = st_ref[h]
        ys.append(_dot(att.astype(BF16), vb[:, vs]) + _nt(qs[:, ks], st_h.astype(BF16)))
        st_ref[h] = st_h * dec[:, ks] + _tn(vb[:, vs], ke[:, ks])
    y = jnp.concatenate(ys, axis=1)
    if d == 0:
        o_ref[...] = y
    else:
        y = y0_ref[...] + y
        parts = []
        for h in range(GLA_HEADS):
            yh = y[:, h * dv:(h + 1) * dv]
            parts.append(yh * lax.rsqrt(jnp.mean(yh * yh, axis=-1, keepdims=True) + EPS))
        y = jnp.concatenate(parts, axis=1) * nw_ref[...]
        o_ref[...] = (y * _silu(g_ref[...].astype(F32))).astype(o_ref.dtype)


def gla_mixer(p, psmall, w2p, b2, normw, n_lat_rows):
    bsz, t, _ = p.shape
    cs = GLA_CHUNK
    w = BRANCH_W
    hw = GLA_HEADS * GLA_HDK
    nct, ncl = t // cs, n_lat_rows // cs
    outs = None
    for d in range(2):
        cm_ = _scan_chunk_map(d, ncl, nct)
        row = lambda width, cb, cm_=cm_: pl.BlockSpec((None, cs, width), lambda bi, ti: (bi, cm_(ti), cb))
        const = lambda shape: pl.BlockSpec(shape, lambda bi, ti: (0,) * len(shape))
        in_specs = [row(hw, C_GQ // hw), row(hw, C_GK // hw), row(w, C_GV // w), row(NSMALL, 0),
                    pl.BlockSpec((None, NSMALL, hw), lambda bi, ti, d=d: (d, 0, 0)),
                    pl.BlockSpec((None, 1, hw), lambda bi, ti, d=d: (d, 0, 0))]
        args = [p, p, p, psmall, w2p, b2]
        if d == 1:
            in_specs += [row(w, 0), row(w, C_GG // w), const((1, w))]
            args += [outs, p, normw]
        outs = pl.pallas_call(
            functools.partial(_gla_kernel, direction=d), grid=(bsz, nct),
            in_specs=in_specs, out_specs=row(w, 0),
            out_shape=jax.ShapeDtypeStruct((bsz, t, w), F32 if d == 0 else BF16),
            scratch_shapes=[pltpu.VMEM((GLA_HEADS, GLA_HDV, GLA_HDK), F32)],
            compiler_params=_cp("arbitrary", "arbitrary"), name=f"gla_scan_dir{d}",
        )(*args)
    return outs


def _rope(x, cos, sin_signed):
    half = RET_HD // 2
    lo, hi = x[:, :half], x[:, half:]
    lo = lo * cos[:, :half] + pltpu.roll(lo, half // 2, 1) * sin_signed[:, :half]
    hi = hi * cos[:, half:] + pltpu.roll(hi, half // 2, 1) * sin_signed[:, half:]
    return jnp.concatenate([lo, hi], axis=1)


def _ret_kernel(*refs, direction):
    d = direction
    if d == 0:
        q_ref, k_ref, v_ref, cos_ref, sin_ref, dr_ref, o_ref, st_ref = refs
    else:
        q_ref, k_ref, v_ref, cos_ref, sin_ref, dr_ref, y0_ref, g_ref, o_ref, st_ref = refs
    cs = RET_CHUNK
    hdim = RET_HD

    @pl.when(pl.program_id(1) == 0)
    def _():
        st_ref[...] = jnp.zeros_like(st_ref)

    lam_all = -jnp.exp(dr_ref[...])
    tri = _tri_mask(cs, d)
    ri = lax.broadcasted_iota(jnp.int32, (cs, cs), 0)
    ci = lax.broadcasted_iota(jnp.int32, (cs, cs), 1)
    lag = jnp.abs(ri - ci).astype(F32)
    pos = lax.broadcasted_iota(jnp.int32, (cs, hdim), 0).astype(F32)
    steps_in = (pos + 1.0) if d == 0 else (cs - pos)
    steps_out = (cs - 1.0 - pos) if d == 0 else pos
    cos, sin_s = cos_ref[...], sin_ref[...]
    qf = q_ref[...].astype(F32)
    kf = k_ref[...].astype(F32) * (hdim ** -0.5)
    vb = v_ref[...].astype(BF16)
    ys = []
    for h in range(RET_HEADS):
        hs = slice(h * hdim, (h + 1) * hdim)
        lam = lam_all[d:d + 1, h:h + 1]
        qh = _rope(qf[:, hs], cos, sin_s)
        kh = _rope(kf[:, hs], cos, sin_s)
        qb = qh.astype(BF16)
        dm = jnp.where(tri, jnp.exp(lam * lag), 0.0)
        att = (_nt(qb, kh.astype(BF16)) * dm).astype(BF16)
        st_h = st_ref[h]
        ys.append(_dot(att, vb[:, hs]) + _dot(qb, st_h.astype(BF16)) * jnp.exp(lam * steps_in))
        ke = (kh * jnp.exp(lam * steps_out)).astype(BF16)
        st_ref[h] = st_h * jnp.exp(lam * cs) + _tn(ke, vb[:, hs])
    y = jnp.concatenate(ys, axis=1)
    if d == 0:
        o_ref[...] = y
    else:
        y = y0_ref[...] + y
        y = jnp.concatenate([_ln_rows(y[:, h * hdim:(h + 1) * hdim]) for h in range(RET_HEADS)], axis=1)
        o_ref[...] = (y * _silu(g_ref[...].astype(F32))).astype(o_ref.dtype)


def ret_mixer(p, cos_t, sin_t, decay_pad, n_lat_rows):
    bsz, t, _ = p.shape
    cs = RET_CHUNK
    w = BRANCH_W
    nct, ncl = t // cs, n_lat_rows // cs
    outs = None
    for d in range(2):
        cm_ = _scan_chunk_map(d, ncl, nct)
        row = lambda cb, cm_=cm_: pl.BlockSpec((None, cs, w), lambda bi, ti: (bi, cm_(ti), cb))
        tab = pl.BlockSpec((cs, RET_HD), lambda bi, ti, cm_=cm_: (cm_(ti), 0))
        in_specs = [row(C_RQ // w), row(C_RK // w), row(C_RV // w), tab, tab,
                    pl.BlockSpec((8, LANE), lambda bi, ti: (0, 0))]
        args = [p, p, p, cos_t, sin_t, decay_pad]
        if d == 1:
            in_specs += [row(0), row(C_RG // w)]
            args += [outs, p]
        outs = pl.pallas_call(
            functools.partial(_ret_kernel, direction=d), grid=(bsz, nct),
            in_specs=in_specs, out_specs=row(0),
            out_shape=jax.ShapeDtypeStruct((bsz, t, w), F32 if d == 0 else BF16),
            scratch_shapes=[pltpu.VMEM((RET_HEADS, RET_HD, RET_HD), F32)],
            compiler_params=_cp("arbitrary", "arbitrary"), name=f"ret_scan_dir{d}",
        )(*args)
    return outs


def _rope_tables(n_lat_rows, lc):
    half = RET_HD // 2
    inv = ROPE_BASE ** (-jnp.arange(0, half, 2, dtype=F32) / half)
    tpos = jnp.arange(n_lat_rows)
    row = (tpos // GRID_W).astype(F32)[:, None] * inv[None, :]
    col = (tpos % GRID_W).astype(F32)[:, None] * inv[None, :]
    cos = jnp.concatenate([jnp.cos(row)] * 2 + [jnp.cos(col)] * 2, axis=1)
    sin = jnp.concatenate([-jnp.sin(row), jnp.sin(row), -jnp.sin(col), jnp.sin(col)], axis=1)
    cos = jnp.concatenate([cos, jnp.ones((lc, RET_HD), F32)], axis=0)
    sin = jnp.concatenate([sin, jnp.zeros((lc, RET_HD), F32)], axis=0)
    return cos, sin


def _pad_lanes(v, start):
    v = v.reshape(-1).astype(F32)
    return jnp.zeros((1, NSMALL), F32).at[0, start:start + v.shape[0]].set(v)


def kernel(x, c, ctx, c_ctx, w_ada, b_ada, w_in, hy_conv_w, hy_conv_b, hy_w1, hy_b1, hy_w2, hy_b2, hy_w3,
           hy_b3, hy_w4, hy_freq, hy_skip, ssd_conv_w, ssd_conv_b, ssd_a_log, ssd_dt_bias, ssd_d, ssd_norm_w,
           gla_w2, gla_b2, gla_norm_w, ret_decay, w_gate, w_br, w_out, ln_g, ln_b):
    bsz, n_lat, d = x.shape
    lc = ctx.shape[1]
    t = n_lat + lc
    depth = w_in.shape[0]
    w = BRANCH_W
    assert bsz <= 2 and d == D_MODEL and n_lat % 256 == 0 and lc % 256 == 0

    s = jnp.concatenate([x, ctx], axis=1)
    cs = jnp.zeros((8, d), F32).at[:bsz].set(c).at[2].set(c_ctx)
    mod = ada_modulation(cs, w_ada, b_ada).reshape(depth, 8, 1, 3 * d)

    cos_t, sin_t = _rope_tables(n_lat, lc)
    tables = _dft_tables(2 * n_lat // DFT_MINOR)
    deltas = jnp.abs(jnp.linspace(math.log(HY_TARGET) / HY_SLOW, math.log(HY_TARGET) / HY_FAST, w,
                                  dtype=F32)).reshape(1, w)
    head_of_ch = jnp.arange(w) // SSD_HEADDIM
    expand = jnp.stack([(jnp.arange(NSMALL)[:, None] == (SSD_HEADS * dd + head_of_ch)[None, :]).astype(F32)
                        for dd in range(2)])

    h = ln_modulate(s, mod[0], None, None, n_lat, pre_ln=False, emit_h=True)[0]
    for l in range(depth):
        wi = w_in[l]
        w_main = jnp.concatenate([wi[:, 0:6144], wi[:, 6176:9248], wi[:, 9280:14400]], axis=1).astype(BF16)
        w_small = jnp.concatenate([wi[:, 6144:6176], wi[:, 9248:9280], jnp.zeros((d, NSMALL - 64), F32)],
                                  axis=1).astype(BF16)
        h2 = h.reshape(bsz * t, d)
        p = matmul(h2, w_main, F32, "in_projection").reshape(bsz, t, NP)
        psmall = matmul(h2, w_small, F32, "in_projection_small").reshape(bsz, t, NSMALL)

        hv = short_conv(p, C_HYIN, 3 * w, hy_conv_w[l], hy_conv_b[l], n_lat, False, F32, "hyena_short_conv")
        w1p = jnp.zeros((LANE, HY_HID), F32).at[:2 * HY_BANDS + 1].set(hy_w1[l])
        w4d = hy_w4[l].reshape(HY_HID, 2, 2, w).transpose(2, 0, 1, 3).reshape(2, HY_HID, 2 * w)
        fargs = (w1p, hy_b1[l].reshape(1, -1), hy_w2[l], hy_b2[l].reshape(1, -1), hy_w3[l],
                 hy_b3[l].reshape(1, -1), hy_freq[l], w4d, deltas)
        circ_l = hyena_filter(n_lat, *fargs)
        circ_c = hyena_filter(lc, *fargs)
        y_hy = jnp.concatenate([hyena_latent(hv, p, circ_l, hy_skip[l], n_lat, tables),
                                hyena_context(hv, p, circ_c, hy_skip[l], n_lat, lc)], axis=1)

        xbc = short_conv(p, C_XBC, 2 * w, ssd_conv_w[l], ssd_conv_b[l], n_lat, True, F32, "ssd_short_conv")
        y_ssd = ssd_mixer(xbc, p, psmall, _pad_lanes(ssd_dt_bias[l], 0), _pad_lanes(ssd_a_log[l], 0), expand,
                          jnp.repeat(ssd_d[l].astype(F32), SSD_HEADDIM).reshape(1, w),
                          ssd_norm_w[l].astype(F32).reshape(1, w), n_lat)

        hw = GLA_HEADS * GLA_HDK
        w2p = jnp.zeros((2, NSMALL, hw), F32)
        for dd in range(2):
            w2p = w2p.at[dd, 32 + GLA_RANK * dd:32 + GLA_RANK * (dd + 1)].set(gla_w2[l, dd])
        y_gla = gla_mixer(p, psmall, w2p.astype(BF16), gla_b2[l].astype(F32).reshape(2, 1, hw),
                          jnp.tile(gla_norm_w[l].astype(F32), GLA_HEADS).reshape(1, w), n_lat)

        decay_pad = jnp.zeros((8, LANE), F32).at[:2, :RET_HEADS].set(ret_decay[l])
        y_ret = ret_mixer(p, cos_t, sin_t, decay_pad, n_lat)

        ycat = jnp.concatenate([y_hy, y_ssd, y_gla, y_ret], axis=-1).reshape(bsz * t, 4 * w)
        m = gated_merge(h2, ycat, w_gate[l].astype(BF16), w_br[l].astype(BF16))
        gate_rows = mod[l][:, :, 2 * d:3 * d]
        pre = out_projection(m, w_out[l].astype(BF16), s.reshape(bsz * t, d), gate_rows, t, n_lat)
        pre = pre.reshape(bsz, t, d)
        if l + 1 < depth:
            s, h = ln_modulate(pre, mod[l + 1], ln_g[l], ln_b[l], n_lat, pre_ln=True, emit_h=True)
        else:
            s = ln_modulate(pre, None, ln_g[l], ln_b[l], n_lat, pre_ln=True, emit_h=False)[0]
    return s[:, :n_lat]
```

```python
import functools
import math

import jax
import jax.numpy as jnp
from jax import lax
from jax.experimental import pallas as pl
from jax.experimental.pallas import tpu as pltpu

F32 = jnp.float32
BF16 = jnp.bfloat16
HI = lax.Precision.HIGHEST

D_MODEL = 4096
DEPTH = 2
BRANCH_W = 1024
GRID_W = 64
EPS = 1e-6
ALPHA = (2 * DEPTH) ** 0.25

HY_BANDS = 16
HY_HID = 64
HY_TARGET = 1e-2
HY_FAST = 0.3
HY_SLOW = 1.5

SSD_HEADS = 16
SSD_HEADDIM = 64
SSD_GROUPS = 4
SSD_RPG = 4
SSD_STATE = 128
SSD_CHUNK = 128

GLA_HEADS = 4
GLA_HDK = 128
GLA_HDV = 256
GLA_RANK = 16
GLA_NORMALIZER = 16.0
GLA_CHUNK = 64

RET_HEADS = 4
RET_HD = 256
RET_CHUNK = 128
ROPE_BASE = 10000.0

NP = 14336
C_HYIN, C_HYGATE, C_XBC, C_SSDZ = 0, 3072, 4096, 6144
C_GQ, C_GK, C_GV, C_GG = 7168, 7680, 8192, 9216
C_RQ, C_RK, C_RV, C_RG = 10240, 11264, 12288, 13312
NSMALL = 128
LANE = 128
DFT_MINOR = 128
SUB = 8

VMEM_LIMIT = 52 * 1024 * 1024


def _cp(*sem):
    return pltpu.CompilerParams(dimension_semantics=sem, vmem_limit_bytes=VMEM_LIMIT)


def _pick(n, cands):
    for c in cands:
        if n % c == 0:
            return c
    raise ValueError(f"no tile for {n} in {cands}")


def _silu(x):
    return x * jax.nn.sigmoid(x)


def _softplus(x):
    return jnp.maximum(x, 0.0) + jnp.log1p(jnp.exp(-jnp.abs(x)))


def _log_sigmoid(x):
    return jnp.minimum(x, 0.0) - jnp.log1p(jnp.exp(-jnp.abs(x)))


def _nt(a, b):
    return lax.dot_general(a, b, (((1,), (1,)), ((), ())), preferred_element_type=F32)


def _tn(a, b):
    return lax.dot_general(a, b, (((0,), (0,)), ((), ())), preferred_element_type=F32)


def _dot(a, b):
    return jnp.dot(a, b, preferred_element_type=F32)


def _dot_hi(a, b):
    return jnp.dot(a, b, preferred_element_type=F32, precision=HI)


def _ada_kernel(c_ref, w_ref, b_ref, o_ref):
    a = _silu(c_ref[...]).astype(BF16)
    o_ref[...] = _dot(a, w_ref[...].astype(BF16)) + b_ref[...]


def ada_modulation(cs, w_ada, b_ada):
    depth, d, n = w_ada.shape
    tn = 512
    return pl.pallas_call(
        _ada_kernel,
        grid=(depth, n // tn),
        in_specs=[pl.BlockSpec((8, d), lambda l, j: (0, 0)),
                  pl.BlockSpec((None, d, tn), lambda l, j: (l, 0, j)),
                  pl.BlockSpec((None, 1, tn), lambda l, j: (l, 0, j))],
        out_specs=pl.BlockSpec((None, 8, tn), lambda l, j: (l, 0, j)),
        out_shape=jax.ShapeDtypeStruct((depth, 8, n), F32),
        compiler_params=_cp("arbitrary", "arbitrary"),
        name="ada_modulation",
    )(cs, w_ada, b_ada.reshape(depth, 1, n))


def _ln_rows(x):
    xc = x - jnp.mean(x, axis=-1, keepdims=True)
    return xc * lax.rsqrt(jnp.mean(xc * xc, axis=-1, keepdims=True) + EPS)


def _lnmod_kernel(*refs, pre_ln, emit_h, d):
    it = iter(refs)
    s_ref = next(it)
    mod_ref = next(it) if emit_h else None
    g_ref = next(it) if pre_ln else None
    b_ref = next(it) if pre_ln else None
    s_out = next(it) if pre_ln else None
    h_out = next(it) if emit_h else None
    x = s_ref[...]
    if pre_ln:
        x = _ln_rows(x) * g_ref[...] + b_ref[...]
        s_out[...] = x
    if emit_h:
        shift = mod_ref[:, 0:d]
        scale = mod_ref[:, d:2 * d]
        h_out[...] = (_ln_rows(x) * (1.0 + scale) + shift).astype(BF16)


def ln_modulate(s, mod_rows, ln_g, ln_b, n_lat_rows, *, pre_ln, emit_h, rows=None):
    b, _, d = s.shape
    t = s.shape[1] if rows is None else rows
    tr = 256
    nlat = n_lat_rows // tr
    args = [s]
    in_specs = [pl.BlockSpec((None, tr, d), lambda bi, ti: (bi, ti, 0))]
    if emit_h:
        args.append(mod_rows)
        in_specs.append(pl.BlockSpec((None, 1, 3 * d), lambda bi, ti: (jnp.where(ti < nlat, bi, 2), 0, 0)))
    if pre_ln:
        args += [ln_g.reshape(1, d), ln_b.reshape(1, d)]
        in_specs += [pl.BlockSpec((1, d), lambda bi, ti: (0, 0))] * 2
    out_shape, out_specs = [], []
    if pre_ln:
        out_shape.append(jax.ShapeDtypeStruct((b, t, d), F32))
        out_specs.append(pl.BlockSpec((None, tr, d), lambda bi, ti: (bi, ti, 0)))
    if emit_h:
        out_shape.append(jax.ShapeDtypeStruct((b, t, d), BF16))
        out_specs.append(pl.BlockSpec((None, tr, d), lambda bi, ti: (bi, ti, 0)))
    return pl.pallas_call(
        functools.partial(_lnmod_kernel, pre_ln=pre_ln, emit_h=emit_h, d=d),
        grid=(b, t // tr), in_specs=in_specs, out_specs=out_specs, out_shape=out_shape,
        compiler_params=_cp("arbitrary", "arbitrary"), name="ln_modulate",
    )(*args)


def _mm_kernel(a_ref, w_ref, o_ref):
    o_ref[...] = _dot(a_ref[...], w_ref[...]).astype(o_ref.dtype)


def matmul(a, w, out_dtype, name):
    m, k = a.shape
    n = w.shape[1]
    tm = _pick(m, (768, 512, 256))
    tn = _pick(n, (1024, 512, 128))
    return pl.pallas_call(
        _mm_kernel, grid=(m // tm, n // tn),
        in_specs=[pl.BlockSpec((tm, k), lambda i, j: (i, 0)),
                  pl.BlockSpec((k, tn), lambda i, j: (0, j))],
        out_specs=pl.BlockSpec((tm, tn), lambda i, j: (i, j)),
        out_shape=jax.ShapeDtypeStruct((m, n), out_dtype),
        compiler_params=_cp("arbitrary", "arbitrary"), name=name,
    )(a, w)


def _merge_kernel(h_ref, y_ref, wg_ref, wb_ref, o_ref, acc_ref):
    i = pl.program_id(2)
    term = jax.nn.sigmoid(_dot(h_ref[...], wg_ref[...])) * _dot(y_ref[...], wb_ref[...])

    @pl.when(i == 0)
    def _():
        acc_ref[...] = term

    @pl.when(i > 0)
    def _():
        acc_ref[...] += term

    @pl.when(i == pl.num_programs(2) - 1)
    def _():
        o_ref[...] = acc_ref[...].astype(o_ref.dtype)


def gated_merge(h, y, w_gate, w_br):
    m, d = h.shape
    nb, bw, _ = w_br.shape
    tm = _pick(m, (768, 512, 256))
    tn = 512
    return pl.pallas_call(
        _merge_kernel, grid=(m // tm, d // tn, nb),
        in_specs=[pl.BlockSpec((tm, d), lambda i, j, r: (i, 0)),
                  pl.BlockSpec((tm, bw), lambda i, j, r: (i, r)),
                  pl.BlockSpec((None, d, tn), lambda i, j, r: (r, 0, j)),
                  pl.BlockSpec((None, bw, tn), lambda i, j, r: (r, 0, j))],
        out_specs=pl.BlockSpec((tm, tn), lambda i, j, r: (i, j)),
        out_shape=jax.ShapeDtypeStruct((m, d), BF16),
        scratch_shapes=[pltpu.VMEM((tm, tn), F32)],
        compiler_params=_cp("arbitrary", "arbitrary", "arbitrary"), name="gated_merge",
    )(h, y, w_gate, w_br)


def _outproj_kernel(m_ref, w_ref, s_ref, gl_ref, gc_ref, o_ref, *, tm, rows_per_batch, n_lat_rows):
    out = _dot(m_ref[...], w_ref[...])
    row = pl.program_id(0) * tm + lax.broadcasted_iota(jnp.int32, (tm, 1), 0)
    is_lat = (row % rows_per_batch) < n_lat_rows
    gate = jnp.where(is_lat, gl_ref[...], gc_ref[...])
    o_ref[...] = ALPHA * s_ref[...] + gate * out


def out_projection(m2, w_out, s2, gate_rows, rows_per_batch, n_lat_rows):
    m, d = m2.shape
    tm = _pick(rows_per_batch, (768, 512, 256))
    tn = 1024
    bpb = rows_per_batch // tm
    return pl.pallas_call(
        functools.partial(_outproj_kernel, tm=tm, rows_per_batch=rows_per_batch, n_lat_rows=n_lat_rows),
        grid=(m // tm, d // tn),
        in_specs=[pl.BlockSpec((tm, d), lambda i, j: (i, 0)),
                  pl.BlockSpec((d, tn), lambda i, j: (0, j)),
                  pl.BlockSpec((tm, tn), lambda i, j: (i, j)),
                  pl.BlockSpec((None, 1, tn), lambda i, j: (i // bpb, 0, j)),
                  pl.BlockSpec((None, 1, tn), lambda i, j: (2, 0, j))],
        out_specs=pl.BlockSpec((tm, tn), lambda i, j: (i, j)),
        out_shape=jax.ShapeDtypeStruct((m, d), F32),
        compiler_params=_cp("arbitrary", "arbitrary"), name="out_projection",
    )(m2, w_out, s2, gate_rows, gate_rows)


CONV_ROWS = 256


def _conv_kernel(x_ref, w_ref, b_ref, o_ref, *, t_rows, n_lat_rows, act):
    r = CONV_ROWS
    w0, w1, w2 = w_ref[0:1, :], w_ref[1:2, :], w_ref[2:3, :]
    bias = b_ref[...]
    rid = lax.broadcasted_iota(jnp.int32, (r, 1), 0)

    def body(i, carry):
        r0 = pl.multiple_of(i * r, r)
        cur = x_ref[pl.ds(r0, r), :].astype(F32)
        p0 = pl.multiple_of(jnp.maximum(r0 - 8, 0), 8)
        n0 = pl.multiple_of(jnp.minimum(r0 + r, t_rows - 8), 8)
        prev_row = x_ref[pl.ds(p0, 8), :].astype(F32)[7:8, :]
        next_row = x_ref[pl.ds(n0, 8), :].astype(F32)[0:1, :]
        up = jnp.where(rid == 0, prev_row, pltpu.roll(cur, 1, 0))
        dn = jnp.where(rid == r - 1, next_row, pltpu.roll(cur, r - 1, 0))
        gpos = r0 + rid
        up = jnp.where((gpos == 0) | (gpos == n_lat_rows), 0.0, up)
        dn = jnp.where((gpos == n_lat_rows - 1) | (gpos == t_rows - 1), 0.0, dn)
        y = w0 * up + w1 * cur + w2 * dn + bias
        if act:
            y = _silu(y)
        o_ref[pl.ds(r0, r), :] = y.astype(o_ref.dtype)
        return carry

    lax.fori_loop(0, t_rows // r, body, 0)


def short_conv(p, col0, width, w, b, n_lat_rows, act, out_dtype, name):
    bsz, t, _ = p.shape
    ct = 256
    cb = col0 // ct
    return pl.pallas_call(
        functools.partial(_conv_kernel, t_rows=t, n_lat_rows=n_lat_rows, act=act),
        grid=(bsz, width // ct),
        in_specs=[pl.BlockSpec((None, t, ct), lambda bi, j: (bi, 0, cb + j)),
                  pl.BlockSpec((3, ct), lambda bi, j: (0, j)),
                  pl.BlockSpec((1, ct), lambda bi, j: (0, j))],
        out_specs=pl.BlockSpec((None, t, ct), lambda bi, j: (bi, 0, j)),
        out_shape=jax.ShapeDtypeStruct((bsz, t, width), out_dtype),
        compiler_params=_cp("arbitrary", "arbitrary"), name=name,
    )(p, w, b.reshape(1, width))


def _filter_kernel(w1_ref, b1_ref, w2_ref, b2_ref, w3_ref, b3_ref, fr_ref, w4_ref, dl_ref, o_ref, *, seq, tr):
    n = pl.program_id(0) * tr + lax.broadcasted_iota(jnp.int32, (tr, 1), 0)
    lag = jnp.where(n < seq, n, 2 * seq - n).astype(F32)
    t = lag * (1.0 / seq)
    lane = lax.broadcasted_iota(jnp.int32, (1, LANE), 1)
    band = jnp.where(lane <= HY_BANDS, lane, lane - HY_BANDS).astype(F32)
    arg = (jnp.float32(2.0 * math.pi) * band) * t
    z = jnp.where(lane == 0, t,
                  jnp.where(lane <= HY_BANDS, jnp.cos(arg),
                            jnp.where(lane <= 2 * HY_BANDS, jnp.sin(arg), 0.0)))
    hdn = jnp.sin(fr_ref[0:1, :] * (_dot_hi(z, w1_ref[...]) + b1_ref[...]))
    hdn = jnp.sin(fr_ref[1:2, :] * (_dot_hi(hdn, w2_ref[...]) + b2_ref[...]))
    hdn = jnp.sin(fr_ref[2:3, :] * (_dot_hi(hdn, w3_ref[...]) + b3_ref[...]))
    filt = _dot_hi(hdn, w4_ref[...])
    win = jnp.exp(-t * dl_ref[...])
    win = jnp.where(n == seq, 0.0, win)
    wdt = o_ref.shape[-1]
    o_ref[0] = filt[:, :wdt] * win
    o_ref[1] = filt[:, wdt:] * win


def hyena_filter(seq, w1p, b1, w2, b2, w3, b3, freq, w4d, deltas_abs):
    tr = 256
    wdt = deltas_abs.shape[-1]
    nblk = 2 * seq // tr
    half = seq // tr
    full = lambda shape: pl.BlockSpec(shape, lambda i: (0,) * len(shape))
    return pl.pallas_call(
        functools.partial(_filter_kernel, seq=seq, tr=tr),
        grid=(nblk,),
        in_specs=[full((LANE, HY_HID)), full((1, HY_HID)), full((HY_HID, HY_HID)), full((1, HY_HID)),
                  full((HY_HID, HY_HID)), full((1, HY_HID)), full((3, HY_HID)),
                  pl.BlockSpec((None, HY_HID, 2 * wdt), lambda i: (jnp.where(i < half, 0, 1), 0, 0)),
                  full((1, wdt))],
        out_specs=pl.BlockSpec((2, tr, wdt), lambda i: (0, i, 0)),
        out_shape=jax.ShapeDtypeStruct((2, 2 * seq, wdt), F32),
        compiler_params=_cp("arbitrary"), name="hyena_filter",
    )(w1p, b1, w2, b2, w3, b3, freq, w4d, deltas_abs)


def _dft_tables(n1):
    n = n1 * DFT_MINOR
    two_pi = 2.0 * math.pi
    ia = jnp.arange(n1, dtype=jnp.int32)
    ib = jnp.arange(DFT_MINOR, dtype=jnp.int32)
    m = (ia[None, :, None] * (DFT_MINOR * ia[None, None, :] + ib[:, None, None])) % n
    ang = m.astype(F32) * (two_pi / n)
    g = jnp.concatenate([jnp.cos(ang), -jnp.sin(ang)], axis=1).astype(BF16)
    th = ((ib[:, None] * ib[None, :]) % DFT_MINOR).astype(F32) * (two_pi / DFT_MINOR)
    c, s = jnp.cos(th), jnp.sin(th)
    mf = jnp.concatenate([jnp.concatenate([c, s], 1), jnp.concatenate([-s, c], 1)], 0).astype(BF16)
    kk = ia[:, None, None] + n1 * ib[None, None, :]
    mi = (ib[None, :, None] * kk) % n
    ps = mi.astype(F32) * (two_pi / n)
    cr, ci = jnp.cos(ps), jnp.sin(ps)
    cinv = jnp.concatenate([jnp.concatenate([cr, -ci], 2), jnp.concatenate([ci, cr], 2)], 1).astype(BF16)
    ph = ((ia[: n1 // 2, None] * ia[None, :]) % n1).astype(F32) * (two_pi / n1)
    m3 = (jnp.concatenate([jnp.cos(ph), -jnp.sin(ph)], 1) * (1.0 / n)).astype(BF16)
    return g, mf, cinv, m3


def _s1_kernel(g_ref, x_ref, o_ref):
    for j in range(SUB):
        o_ref[:, j, :] = _dot(g_ref[j], x_ref[:, j, :].astype(BF16))


def dft_stage1(x4, g, comp, rows, wdt, name):
    gsz = x4.shape[0]
    n1x2 = g.shape[1]
    return pl.pallas_call(
        _s1_kernel, grid=(gsz, DFT_MINOR // SUB),
        in_specs=[pl.BlockSpec((SUB, n1x2, rows), lambda i, bb: (bb, 0, 0)),
                  pl.BlockSpec((None, rows, SUB, wdt), lambda i, bb: (i, 0, bb, comp))],
        out_specs=pl.BlockSpec((None, n1x2, SUB, wdt), lambda i, bb: (i, 0, bb, 0)),
        out_shape=jax.ShapeDtypeStruct((gsz, n1x2, DFT_MINOR, wdt), F32),
        compiler_params=_cp("arbitrary", "arbitrary"), name=name,
    )(g, x4)


def _s2f_kernel(mf_ref, a_ref, o_ref):
    wdt = a_ref.shape[-1]
    for j in range(SUB):
        y = _dot(mf_ref[...], a_ref[:, j].astype(BF16).reshape(2 * DFT_MINOR, wdt))
        o_ref[:, j] = y.reshape(2, DFT_MINOR, wdt)


def filter_spectrum(a5, mf):
    no, _, n1, _, wdt = a5.shape
    ct = 512
    spec = pl.BlockSpec((None, 2, SUB, DFT_MINOR, ct), lambda o, c, jc: (o, 0, c, 0, jc))
    return pl.pallas_call(
        _s2f_kernel, grid=(no, n1 // SUB, wdt // ct),
        in_specs=[pl.BlockSpec((2 * DFT_MINOR, 2 * DFT_MINOR), lambda o, c, jc: (0, 0)), spec],
        out_specs=spec,
        out_shape=jax.ShapeDtypeStruct(a5.shape, F32),
        compiler_params=_cp("arbitrary", "arbitrary", "arbitrary"), name="hyena_filter_spectrum",
    )(mf, a5)


def _s2_kernel(mf_ref, ci_ref, h_ref, a_ref, o_ref):
    wdt = a_ref.shape[-1]
    for j in range(SUB):
        y = _dot(mf_ref[...], a_ref[:, j].astype(BF16).reshape(2 * DFT_MINOR, wdt))
        yr, yi = y[:DFT_MINOR], y[DFT_MINOR:]
        hr, hi = h_ref[0, j], h_ref[1, j]
        z = jnp.concatenate([yr * hr - yi * hi, yr * hi + yi * hr], axis=0).astype(BF16)
        o_ref[:, j, :] = _dot(ci_ref[j], z)


def spectral_multiply(a5, spec5, order, mf, cinv):
    bsz, _, n1, _, wdt = a5.shape
    ct = 512
    return pl.pallas_call(
        _s2_kernel, grid=(n1 // SUB, bsz, wdt // ct),
        in_specs=[pl.BlockSpec((2 * DFT_MINOR, 2 * DFT_MINOR), lambda c, bi, jc: (0, 0)),
                  pl.BlockSpec((SUB, 2 * DFT_MINOR, 2 * DFT_MINOR), lambda c, bi, jc: (c, 0, 0)),
                  pl.BlockSpec((None, 2, SUB, DFT_MINOR, ct), lambda c, bi, jc: (order, 0, c, 0, jc)),
                  pl.BlockSpec((None, 2, SUB, DFT_MINOR, ct), lambda c, bi, jc: (bi, 0, c, 0, jc))],
        out_specs=pl.BlockSpec((None, 2 * DFT_MINOR, SUB, ct), lambda c, bi, jc: (bi, 0, c, jc)),
        out_shape=jax.ShapeDtypeStruct((bsz, 2 * DFT_MINOR, n1, wdt), F32),
        compiler_params=_cp("arbitrary", "arbitrary", "arbitrary"), name="hyena_spectral_multiply",
    )(mf, cinv, spec5, a5)


def _s3_mid_kernel(m3_ref, g_ref, bq_ref, z_ref, x_ref, sk_ref, zo_ref, ao_ref):
    wdt = bq_ref.shape[-1]
    for j in range(SUB):
        y = _dot(m3_ref[...], bq_ref[:, j].astype(BF16).reshape(-1, wdt))
        z = x_ref[:, j, :] * (y + z_ref[:, j, :] * sk_ref[...])
        zo_ref[:, j, :] = z
        ao_ref[:, j, :] = _dot(g_ref[j], z.astype(BF16))


def _s3_last_kernel(m3_ref, bq_ref, z_ref, x_ref, sk_ref, gate_ref, o_ref):
    wdt = bq_ref.shape[-1]
    for j in range(SUB):
        y = _dot(m3_ref[...], bq_ref[:, j].astype(BF16).reshape(-1, wdt))
        z = x_ref[:, j, :] * (y + z_ref[:, j, :] * sk_ref[...])
        o_ref[:, j, :] = z * _silu(gate_ref[:, j, :])


def _hyena_ctx_kernel(ff_ref, fd_ref, fi_ref, circ_ref, v_ref, x1_ref, x2_ref, gate_ref, sk_ref, o_ref, *, nc):
    z = v_ref[...].astype(F32)
    xs = (x1_ref, x2_ref)
    for o in range(2):
        hs = _dot(ff_ref[...], circ_ref[o].astype(BF16))
        us = _dot(fd_ref[...], z.astype(BF16))
        hr, hi = hs[:nc], hs[nc:]
        ur, ui = us[:nc], us[nc:]
        zz = jnp.concatenate([ur * hr - ui * hi, ur * hi + ui * hr], axis=0).astype(BF16)
        y = _dot(fi_ref[...], zz)
        z = xs[o][...].astype(F32) * (y + z * sk_ref[o:o + 1, :])
    o_ref[...] = (z * _silu(gate_ref[...].astype(F32))).astype(o_ref.dtype)


def hyena_context(hv, p, circ_c, skip, n_lat_rows, lc):
    bsz, _, w3 = hv.shape
    wdt = w3 // 3
    nc = 2 * lc
    ct = 256
    two_pi = 2.0 * math.pi
    ik = jnp.arange(nc, dtype=jnp.int32)
    th = ((ik[:, None] * ik[None, :]) % nc).astype(F32) * (two_pi / nc)
    c, s = jnp.cos(th), jnp.sin(th)
    ffull = jnp.concatenate([c, -s], axis=0).astype(BF16)
    fdata = ffull[:, :lc]
    finv = (jnp.concatenate([c[:lc], -s[:lc]], axis=1) * (1.0 / nc)).astype(BF16)
    rb = n_lat_rows // lc
    cw = wdt // ct
    full = lambda shape: pl.BlockSpec(shape, lambda bi, j: (0,) * len(shape))
    return pl.pallas_call(
        functools.partial(_hyena_ctx_kernel, nc=nc),
        grid=(bsz, cw),
        in_specs=[full((2 * nc, nc)), full((2 * nc, lc)), full((lc, 2 * nc)),
                  pl.BlockSpec((2, nc, ct), lambda bi, j: (0, 0, j)),
                  pl.BlockSpec((None, lc, ct), lambda bi, j: (bi, rb, j)),
                  pl.BlockSpec((None, lc, ct), lambda bi, j: (bi, rb, cw + j)),
                  pl.BlockSpec((None, lc, ct), lambda bi, j: (bi, rb, 2 * cw + j)),
                  pl.BlockSpec((None, lc, ct), lambda bi, j: (bi, rb, C_HYGATE // ct + j)),
                  pl.BlockSpec((2, ct), lambda bi, j: (0, j))],
        out_specs=pl.BlockSpec((None, lc, ct), lambda bi, j: (bi, 0, j)),
        out_shape=jax.ShapeDtypeStruct((bsz, lc, wdt), BF16),
        compiler_params=_cp("arbitrary", "arbitrary"), name="hyena_context",
    )(ffull, fdata, finv, circ_c, hv, hv, hv, p, skip)


def hyena_latent(hv, p, circ_l, skip, n_lat_rows, tables):
    g, mf, cinv, m3 = tables
    bsz, t, w3 = hv.shape
    wdt = w3 // 3
    n1 = 2 * n_lat_rows // DFT_MINOR
    half = n1 // 2
    ct = 512
    nj = wdt // ct
    af = dft_stage1(circ_l.reshape(2, n1, DFT_MINOR, wdt), g, 0, n1, wdt, "hyena_filter_stage1")
    spec5 = filter_spectrum(af.reshape(2, 2, n1, DFT_MINOR, wdt), mf)
    gd = g[:, :, :half]
    hv4 = hv.reshape(bsz, t // DFT_MINOR, DFT_MINOR, w3)
    p4 = p.reshape(bsz, t // DFT_MINOR, DFT_MINOR, p.shape[-1])
    sk = skip.reshape(2, 1, wdt)
    a = dft_stage1(hv4, gd, 0, half, wdt, "hyena_stage1")
    bq = spectral_multiply(a.reshape(bsz, 2, n1, DFT_MINOR, wdt), spec5, 0, mf, cinv)
    grid = (bsz, DFT_MINOR // SUB, nj)
    nat = lambda cb: pl.BlockSpec((None, half, SUB, ct), lambda bi, bb, jc: (bi, 0, bb, cb + jc))
    bq_spec = pl.BlockSpec((None, 2, SUB, n1, ct), lambda bi, bb, jc: (bi, 0, bb, 0, jc))
    m3_spec = pl.BlockSpec((half, 2 * n1), lambda bi, bb, jc: (0, 0))
    sk_spec = lambda o: pl.BlockSpec((None, 1, ct), lambda bi, bb, jc: (o, 0, jc))
    z1, a = pl.pallas_call(
        _s3_mid_kernel, grid=grid,
        in_specs=[m3_spec, pl.BlockSpec((SUB, 2 * n1, half), lambda bi, bb, jc: (bb, 0, 0)),
                  bq_spec, nat(0), nat(nj), sk_spec(0)],
        out_specs=[nat(0), pl.BlockSpec((None, 2 * n1, SUB, ct), lambda bi, bb, jc: (bi, 0, bb, jc))],
        out_shape=[jax.ShapeDtypeStruct((bsz, half, DFT_MINOR, wdt), F32),
                   jax.ShapeDtypeStruct((bsz, 2 * n1, DFT_MINOR, wdt), F32)],
        compiler_params=_cp("arbitrary", "arbitrary", "arbitrary"), name="hyena_stage3_mid",
    )(m3, gd, bq.reshape(bsz, 2, DFT_MINOR, n1, wdt), hv4, hv4, sk)
    bq = spectral_multiply(a.reshape(bsz, 2, n1, DFT_MINOR, wdt), spec5, 1, mf, cinv)
    y = pl.pallas_call(
        _s3_last_kernel, grid=grid,
        in_specs=[m3_spec, bq_spec, nat(0), nat(2 * nj), sk_spec(1), nat(C_HYGATE // ct)],
        out_specs=nat(0),
        out_shape=jax.ShapeDtypeStruct((bsz, half, DFT_MINOR, wdt), F32),
        compiler_params=_cp("arbitrary", "arbitrary", "arbitrary"), name="hyena_stage3_last",
    )(m3, bq.reshape(bsz, 2, DFT_MINOR, n1, wdt), z1, hv4, sk, p4)
    return y.reshape(bsz, n_lat_rows, wdt)


def _scan_chunk_map(direction, n_lat_chunks, n_chunks):
    if direction == 0:
        return lambda t: (t + n_lat_chunks) % n_chunks
    return lambda t: n_chunks - 1 - t


def _tri_mask(n, direction):
    ri = lax.broadcasted_iota(jnp.int32, (n, n), 0)
    ci = lax.broadcasted_iota(jnp.int32, (n, n), 1)
    return (ci <= ri) if direction == 0 else (ci >= ri)


def _ssd_kernel(*refs, direction):
    d = direction
    if d == 0:
        xs_ref, bm_ref, cm_ref, sm_ref, dtb_ref, alog_ref, e_ref, o_ref, st_ref = refs
    else:
        (xs_ref, bm_ref, cm_ref, sm_ref, dtb_ref, alog_ref, e_ref,
         y0_ref, z_ref, dsk_ref, nw_ref, o_ref, st_ref) = refs
    q = SSD_CHUNK
    hd = SSD_HEADDIM
    gw = SSD_RPG * hd

    @pl.when(pl.program_id(1) == 0)
    def _():
        st_ref[...] = jnp.zeros_like(st_ref)

    xs = xs_ref[...].astype(F32)
    bm = bm_ref[...].astype(BF16)
    cm = cm_ref[...].astype(BF16)
    dt = _softplus(sm_ref[...] + dtb_ref[...])
    a = dt * (-jnp.exp(alog_ref[...]))
    tri = _tri_mask(q, d)
    acum = _dot_hi(tri.astype(F32), a)
    acum_t = acum.T
    expand = e_ref[...]
    dtx = _dot_hi(dt, expand)
    ax = _dot_hi(acum, expand)
    atx = ax[q - 1:q, :] if d == 0 else ax[0:1, :]
    xdt = xs * dtx
    e_in = jnp.exp(ax)
    xw = (xdt * jnp.exp(atx - ax)).astype(BF16)
    xdt_b = xdt.astype(BF16)
    ys = []
    for g in range(SSD_GROUPS):
        bg = bm[:, g * SSD_STATE:(g + 1) * SSD_STATE]
        cg = cm[:, g * SSD_STATE:(g + 1) * SSD_STATE]
        sc = _nt(cg, bg)
        st_g = st_ref[g * gw:(g + 1) * gw, :]
        y_int = _nt(cg, st_g.astype(BF16))
        yh, decs = [], []
        for r in range(SSD_RPG):
            hl = SSD_HEADS * d + SSD_RPG * g + r
            seg = acum[:, hl:hl + 1] - acum_t[hl:hl + 1, :]
            dm = jnp.exp(jnp.where(tri, seg, -1e30))
            ch = (SSD_RPG * g + r) * hd
            yh.append(_dot((sc * dm).astype(BF16), xdt_b[:, ch:ch + hd]))
            a_tot = acum[q - 1:q, hl:hl + 1] if d == 0 else acum[0:1, hl:hl + 1]
            decs.append(jnp.broadcast_to(jnp.exp(a_tot), (hd, SSD_STATE)))
        ys.append(jnp.concatenate(yh, axis=1) + y_int * e_in[:, g * gw:(g + 1) * gw])
        st_ref[g * gw:(g + 1) * gw, :] = st_g * jnp.concatenate(decs, axis=0) + _tn(xw[:, g * gw:(g + 1) * gw], bg)
    y = jnp.concatenate(ys, axis=1)
    if d == 0:
        o_ref[...] = y
    else:
        y = (y0_ref[...] + y + dsk_ref[...] * xs) * _silu(z_ref[...].astype(F32))
        parts = []
        for g in range(SSD_GROUPS):
            yg = y[:, g * gw:(g + 1) * gw]
            parts.append(yg * lax.rsqrt(jnp.mean(yg * yg, axis=-1, keepdims=True) + EPS))
        o_ref[...] = (jnp.concatenate(parts, axis=1) * nw_ref[...]).astype(o_ref.dtype)


def ssd_mixer(xbc, p, psmall, dtb, alog, expand, dskip, normw, n_lat_rows):
    bsz, t, _ = xbc.shape
    q = SSD_CHUNK
    w = BRANCH_W
    nct, ncl = t // q, n_lat_rows // q
    outs = None
    for d in range(2):
        cm_ = _scan_chunk_map(d, ncl, nct)
        row = lambda width, cb, cm_=cm_: pl.BlockSpec((None, q, width), lambda bi, ti: (bi, cm_(ti), cb))
        const = lambda shape: pl.BlockSpec(shape, lambda bi, ti: (0,) * len(shape))
        in_specs = [row(w, 0), row(4 * SSD_STATE, 2), row(4 * SSD_STATE, 3), row(NSMALL, 0),
                    const((1, NSMALL)), const((1, NSMALL)),
                    pl.BlockSpec((None, NSMALL, w), lambda bi, ti, d=d: (d, 0, 0))]
        args = [xbc, xbc, xbc, psmall, dtb, alog, expand]
        if d == 1:
            in_specs += [row(w, 0), row(w, C_SSDZ // w), const((1, w)), const((1, w))]
            args += [outs, p, dskip, normw]
        outs = pl.pallas_call(
            functools.partial(_ssd_kernel, direction=d), grid=(bsz, nct),
            in_specs=in_specs, out_specs=row(w, 0),
            out_shape=jax.ShapeDtypeStruct((bsz, t, w), F32 if d == 0 else BF16),
            scratch_shapes=[pltpu.VMEM((SSD_HEADS * SSD_HEADDIM, SSD_STATE), F32)],
            compiler_params=_cp("arbitrary", "arbitrary"), name=f"ssd_scan_dir{d}",
        )(*args)
    return outs


def _gla_kernel(*refs, direction):
    d = direction
    if d == 0:
        q_ref, k_ref, v_ref, sm_ref, w2_ref, b2_ref, o_ref, st_ref = refs
    else:
        q_ref, k_ref, v_ref, sm_ref, w2_ref, b2_ref, y0_ref, g_ref, nw_ref, o_ref, st_ref = refs
    cs = GLA_CHUNK
    dk, dv = GLA_HDK, GLA_HDV

    @pl.when(pl.program_id(1) == 0)
    def _():
        st_ref[...] = jnp.zeros_like(st_ref)

    logit = _dot(sm_ref[...].astype(BF16), w2_ref[...]) + b2_ref[...]
    gl = _log_sigmoid(logit) * (1.0 / GLA_NORMALIZER)
    tri = _tri_mask(cs, d)
    gc = _dot_hi(tri.astype(F32), gl)
    mid = cs // 2 if d == 0 else cs - 1 - cs // 2
    g_mid = gc[mid:mid + 1, :]
    g_last = gc[cs - 1:cs, :] if d == 0 else gc[0:1, :]
    qf = q_ref[...].astype(F32) * (dk ** -0.5)
    kf = k_ref[...].astype(F32)
    vb = v_ref[...].astype(BF16)
    qa = (qf * jnp.exp(gc - g_mid)).astype(BF16)
    ka = (kf * jnp.exp(g_mid - gc)).astype(BF16)
    qs = (qf * jnp.exp(gc)).astype(BF16)
    ke = (kf * jnp.exp(g_last - gc)).astype(BF16)
    dec = jnp.exp(g_last)
    ys = []
    for h in range(GLA_HEADS):
        ks, vs = slice(h * dk, (h + 1) * dk), slice(h * dv, (h + 1) * dv)
        att = jnp.where(tri, _nt(qa[:, ks], ka[:, ks]), 0.0)
        st_h = st_ref[h]
        ys.append(_dot(att.astype(BF16), vb[:, vs]) + _nt(qs[:, ks], st_h.astype(BF16)))
        st_ref[h] = st_h * dec[:, ks] + _tn(vb[:, vs], ke[:, ks])
    y = jnp.concatenate(ys, axis=1)
    if d == 0:
        o_ref[...] = y
    else:
        y = y0_ref[...] + y
        parts = []
        for h in range(GLA_HEADS):
            yh = y[:, h * dv:(h + 1) * dv]
            parts.append(yh * lax.rsqrt(jnp.mean(yh * yh, axis=-1, keepdims=True) + EPS))
        y = jnp.concatenate(parts, axis=1) * nw_ref[...]
        o_ref[...] = (y * _silu(g_ref[...].astype(F32))).astype(o_ref.dtype)


def gla_mixer(p, psmall, w2p, b2, normw, n_lat_rows):
    bsz, t, _ = p.shape
    cs = GLA_CHUNK
    w = BRANCH_W
    hw = GLA_HEADS * GLA_HDK
    nct, ncl = t // cs, n_lat_rows // cs
    outs = None
    for d in range(2):
        cm_ = _scan_chunk_map(d, ncl, nct)
        row = lambda width, cb, cm_=cm_: pl.BlockSpec((None, cs, width), lambda bi, ti: (bi, cm_(ti), cb))
        const = lambda shape: pl.BlockSpec(shape, lambda bi, ti: (0,) * len(shape))
        in_specs = [row(hw, C_GQ // hw), row(hw, C_GK // hw), row(w, C_GV // w), row(NSMALL, 0),
                    pl.BlockSpec((None, NSMALL, hw), lambda bi, ti, d=d: (d, 0, 0)),
                    pl.BlockSpec((None, 1, hw), lambda bi, ti, d=d: (d, 0, 0))]
        args = [p, p, p, psmall, w2p, b2]
        if d == 1:
            in_specs += [row(w, 0), row(w, C_GG // w), const((1, w))]
            args += [outs, p, normw]
        outs = pl.pallas_call(
            functools.partial(_gla_kernel, direction=d), grid=(bsz, nct),
            in_specs=in_specs, out_specs=row(w, 0),
            out_shape=jax.ShapeDtypeStruct((bsz, t, w), F32 if d == 0 else BF16),
            scratch_shapes=[pltpu.VMEM((GLA_HEADS, GLA_HDV, GLA_HDK), F32)],
            compiler_params=_cp("arbitrary", "arbitrary"), name=f"gla_scan_dir{d}",
        )(*args)
    return outs


def _rope(x, cos, sin_signed):
    half = RET_HD // 2
    lo, hi = x[:, :half], x[:, half:]
    lo = lo * cos[:, :half] + pltpu.roll(lo, half // 2, 1) * sin_signed[:, :half]
    hi = hi * cos[:, half:] + pltpu.roll(hi, half // 2, 1) * sin_signed[:, half:]
    return jnp.concatenate([lo, hi], axis=1)


def _ret_kernel(*refs, direction):
    d = direction
    if d == 0:
        q_ref, k_ref, v_ref, cos_ref, sin_ref, dr_ref, o_ref, st_ref = refs
    else:
        q_ref, k_ref, v_ref, cos_ref, sin_ref, dr_ref, y0_ref, g_ref, o_ref, st_ref = refs
    cs = RET_CHUNK
    hdim = RET_HD

    @pl.when(pl.program_id(1) == 0)
    def _():
        st_ref[...] = jnp.zeros_like(st_ref)

    lam_all = -jnp.exp(dr_ref[...])
    tri = _tri_mask(cs, d)
    ri = lax.broadcasted_iota(jnp.int32, (cs, cs), 0)
    ci = lax.broadcasted_iota(jnp.int32, (cs, cs), 1)
    lag = jnp.abs(ri - ci).astype(F32)
    pos = lax.broadcasted_iota(jnp.int32, (cs, hdim), 0).astype(F32)
    steps_in = (pos + 1.0) if d == 0 else (cs - pos)
    steps_out = (cs - 1.0 - pos) if d == 0 else pos
    cos, sin_s = cos_ref[...], sin_ref[...]
    qf = q_ref[...].astype(F32)
    kf = k_ref[...].astype(F32) * (hdim ** -0.5)
    vb = v_ref[...].astype(BF16)
    ys = []
    for h in range(RET_HEADS):
        hs = slice(h * hdim, (h + 1) * hdim)
        lam = lam_all[d:d + 1, h:h + 1]
        qh = _rope(qf[:, hs], cos, sin_s)
        kh = _rope(kf[:, hs], cos, sin_s)
        qb = qh.astype(BF16)
        dm = jnp.where(tri, jnp.exp(lam * lag), 0.0)
        att = (_nt(qb, kh.astype(BF16)) * dm).astype(BF16)
        st_h = st_ref[h]
        ys.append(_dot(att, vb[:, hs]) + _dot(qb, st_h.astype(BF16)) * jnp.exp(lam * steps_in))
        ke = (kh * jnp.exp(lam * steps_out)).astype(BF16)
        st_ref[h] = st_h * jnp.exp(lam * cs) + _tn(ke, vb[:, hs])
    y = jnp.concatenate(ys, axis=1)
    if d == 0:
        o_ref[...] = y
    else:
        y = y0_ref[...] + y
        y = jnp.concatenate([_ln_rows(y[:, h * hdim:(h + 1) * hdim]) for h in range(RET_HEADS)], axis=1)
        o_ref[...] = (y * _silu(g_ref[...].astype(F32))).astype(o_ref.dtype)


def ret_mixer(p, cos_t, sin_t, decay_pad, n_lat_rows):
    bsz, t, _ = p.shape
    cs = RET_CHUNK
    w = BRANCH_W
    nct, ncl = t // cs, n_lat_rows // cs
    outs = None
    for d in range(2):
        cm_ = _scan_chunk_map(d, ncl, nct)
        row = lambda cb, cm_=cm_: pl.BlockSpec((None, cs, w), lambda bi, ti: (bi, cm_(ti), cb))
        tab = pl.BlockSpec((cs, RET_HD), lambda bi, ti, cm_=cm_: (cm_(ti), 0))
        in_specs = [row(C_RQ // w), row(C_RK // w), row(C_RV // w), tab, tab,
                    pl.BlockSpec((8, LANE), lambda bi, ti: (0, 0))]
        args = [p, p, p, cos_t, sin_t, decay_pad]
        if d == 1:
            in_specs += [row(0), row(C_RG // w)]
            args += [outs, p]
        outs = pl.pallas_call(
            functools.partial(_ret_kernel, direction=d), grid=(bsz, nct),
            in_specs=in_specs, out_specs=row(0),
            out_shape=jax.ShapeDtypeStruct((bsz, t, w), F32 if d == 0 else BF16),
            scratch_shapes=[pltpu.VMEM((RET_HEADS, RET_HD, RET_HD), F32)],
            compiler_params=_cp("arbitrary", "arbitrary"), name=f"ret_scan_dir{d}",
        )(*args)
    return outs


def _rope_tables(n_lat_rows, lc):
    half = RET_HD // 2
    inv = ROPE_BASE ** (-jnp.arange(0, half, 2, dtype=F32) / half)
    tpos = jnp.arange(n_lat_rows)
    row = (tpos // GRID_W).astype(F32)[:, None] * inv[None, :]
    col = (tpos % GRID_W).astype(F32)[:, None] * inv[None, :]
    cos = jnp.concatenate([jnp.cos(row)] * 2 + [jnp.cos(col)] * 2, axis=1)
    sin = jnp.concatenate([-jnp.sin(row), jnp.sin(row), -jnp.sin(col), jnp.sin(col)], axis=1)
    cos = jnp.concatenate([cos, jnp.ones((lc, RET_HD), F32)], axis=0)
    sin = jnp.concatenate([sin, jnp.zeros((lc, RET_HD), F32)], axis=0)
    return cos, sin


def _pad_lanes(v, start):
    v = v.reshape(-1).astype(F32)
    return jnp.zeros((1, NSMALL), F32).at[0, start:start + v.shape[0]].set(v)


def kernel(x, c, ctx, c_ctx, w_ada, b_ada, w_in, hy_conv_w, hy_conv_b, hy_w1, hy_b1, hy_w2, hy_b2, hy_w3,
           hy_b3, hy_w4, hy_freq, hy_skip, ssd_conv_w, ssd_conv_b, ssd_a_log, ssd_dt_bias, ssd_d, ssd_norm_w,
           gla_w2, gla_b2, gla_norm_w, ret_decay, w_gate, w_br, w_out, ln_g, ln_b):
    bsz, n_lat, d = x.shape
    lc = ctx.shape[1]
    t = n_lat + lc
    depth = w_in.shape[0]
    w = BRANCH_W
    assert bsz <= 2 and d == D_MODEL and n_lat % 256 == 0 and lc % 256 == 0

    s = jnp.concatenate([x, ctx], axis=1)
    cs = jnp.zeros((8, d), F32).at[:bsz].set(c).at[2].set(c_ctx)
    mod = ada_modulation(cs, w_ada, b_ada).reshape(depth, 8, 1, 3 * d)

    cos_t, sin_t = _rope_tables(n_lat, lc)
    tables = _dft_tables(2 * n_lat // DFT_MINOR)
    deltas = jnp.abs(jnp.linspace(math.log(HY_TARGET) / HY_SLOW, math.log(HY_TARGET) / HY_FAST, w,
                                  dtype=F32)).reshape(1, w)
    head_of_ch = jnp.arange(w) // SSD_HEADDIM
    expand = jnp.stack([(jnp.arange(NSMALL)[:, None] == (SSD_HEADS * dd + head_of_ch)[None, :]).astype(F32)
                        for dd in range(2)])

    h = ln_modulate(s, mod[0], None, None, n_lat, pre_ln=False, emit_h=True)[0]
    for l in range(depth):
        wi = w_in[l]
        w_main = jnp.concatenate([wi[:, 0:6144], wi[:, 6176:9248], wi[:, 9280:14400]], axis=1).astype(BF16)
        w_small = jnp.concatenate([wi[:, 6144:6176], wi[:, 9248:9280], jnp.zeros((d, NSMALL - 64), F32)],
                                  axis=1).astype(BF16)
        h2 = h.reshape(bsz * t, d)
        p = matmul(h2, w_main, F32, "in_projection").reshape(bsz, t, NP)
        psmall = matmul(h2, w_small, F32, "in_projection_small").reshape(bsz, t, NSMALL)

        hv = short_conv(p, C_HYIN, 3 * w, hy_conv_w[l], hy_conv_b[l], n_lat, False, F32, "hyena_short_conv")
        w1p = jnp.zeros((LANE, HY_HID), F32).at[:2 * HY_BANDS + 1].set(hy_w1[l])
        w4d = hy_w4[l].reshape(HY_HID, 2, 2, w).transpose(2, 0, 1, 3).reshape(2, HY_HID, 2 * w)
        fargs = (w1p, hy_b1[l].reshape(1, -1), hy_w2[l], hy_b2[l].reshape(1, -1), hy_w3[l],
                 hy_b3[l].reshape(1, -1), hy_freq[l], w4d, deltas)
        circ_l = hyena_filter(n_lat, *fargs)
        circ_c = hyena_filter(lc, *fargs)
        y_hy = jnp.concatenate([hyena_latent(hv, p, circ_l, hy_skip[l], n_lat, tables).astype(BF16),
                                hyena_context(hv, p, circ_c, hy_skip[l], n_lat, lc)], axis=1)

        xbc = short_conv(p, C_XBC, 2 * w, ssd_conv_w[l], ssd_conv_b[l], n_lat, True, F32, "ssd_short_conv")
        y_ssd = ssd_mixer(xbc, p, psmall, _pad_lanes(ssd_dt_bias[l], 0), _pad_lanes(ssd_a_log[l], 0), expand,
                          jnp.repeat(ssd_d[l].astype(F32), SSD_HEADDIM).reshape(1, w),
                          ssd_norm_w[l].astype(F32).reshape(1, w), n_lat)

        hw = GLA_HEADS * GLA_HDK
        w2p = jnp.zeros((2, NSMALL, hw), F32)
        for dd in range(2):
            w2p = w2p.at[dd, 32 + GLA_RANK * dd:32 + GLA_RANK * (dd + 1)].set(gla_w2[l, dd])
        y_gla = gla_mixer(p, psmall, w2p.astype(BF16), gla_b2[l].astype(F32).reshape(2, 1, hw),
                          jnp.tile(gla_norm_w[l].astype(F32), GLA_HEADS).reshape(1, w), n_lat)

        decay_pad = jnp.zeros((8, LANE), F32).at[:2, :RET_HEADS].set(ret_decay[l])
        y_ret = ret_mixer(p, cos_t, sin_t, decay_pad, n_lat)

        ycat = jnp.concatenate([y_hy, y_ssd, y_gla, y_ret], axis=-1).reshape(bsz * t, 4 * w)
        m = gated_merge(h2, ycat, w_gate[l].astype(BF16), w_br[l].astype(BF16))
        gate_rows = mod[l][:, :, 2 * d:3 * d]
        pre = out_projection(m, w_out[l].astype(BF16), s.reshape(bsz * t, d), gate_rows, t, n_lat)
        pre = pre.reshape(bsz, t, d)
        if l + 1 < depth:
            s, h = ln_modulate(pre, mod[l + 1], ln_g[l], ln_b[l], n_lat, pre_ln=True, emit_h=True)
        else:
            s = ln_modulate(pre, None, ln_g[l], ln_b[l], n_lat, pre_ln=True, emit_h=False, rows=n_lat)[0]
    return s
```

```python
import functools
import math

import jax
import jax.numpy as jnp
from jax import lax
from jax.experimental import pallas as pl
from jax.experimental.pallas import tpu as pltpu

F32 = jnp.float32
BF16 = jnp.bfloat16
HI = lax.Precision.HIGHEST

D_MODEL = 4096
DEPTH = 2
BRANCH_W = 1024
GRID_W = 64
EPS = 1e-6
ALPHA = (2 * DEPTH) ** 0.25

HY_BANDS = 16
HY_HID = 64
HY_TARGET = 1e-2
HY_FAST = 0.3
HY_SLOW = 1.5

SSD_HEADS = 16
SSD_HEADDIM = 64
SSD_GROUPS = 4
SSD_RPG = 4
SSD_STATE = 128
SSD_CHUNK = 128

GLA_HEADS = 4
GLA_HDK = 128
GLA_HDV = 256
GLA_RANK = 16
GLA_NORMALIZER = 16.0
GLA_CHUNK = 64

RET_HEADS = 4
RET_HD = 256
RET_CHUNK = 128
ROPE_BASE = 10000.0

NP = 14336
C_HYIN, C_HYGATE, C_XBC, C_SSDZ = 0, 3072, 4096, 6144
C_GQ, C_GK, C_GV, C_GG = 7168, 7680, 8192, 9216
C_RQ, C_RK, C_RV, C_RG = 10240, 11264, 12288, 13312
NSMALL = 128
LANE = 128
DFT_MINOR = 128
SUB = 8

VMEM_LIMIT = 52 * 1024 * 1024


def _cp(*sem):
    return pltpu.CompilerParams(dimension_semantics=sem, vmem_limit_bytes=VMEM_LIMIT)


def _pick(n, cands):
    for c in cands:
        if n % c == 0:
            return c
    raise ValueError(f"no tile for {n} in {cands}")


def _silu(x):
    return x * jax.nn.sigmoid(x)


def _softplus(x):
    return jnp.maximum(x, 0.0) + jnp.log1p(jnp.exp(-jnp.abs(x)))


def _log_sigmoid(x):
    return jnp.minimum(x, 0.0) - jnp.log1p(jnp.exp(-jnp.abs(x)))


def _nt(a, b):
    return lax.dot_general(a, b, (((1,), (1,)), ((), ())), preferred_element_type=F32)


def _tn(a, b):
    return lax.dot_general(a, b, (((0,), (0,)), ((), ())), preferred_element_type=F32)


def _dot(a, b):
    return jnp.dot(a, b, preferred_element_type=F32)


def _dot_hi(a, b):
    return jnp.dot(a, b, preferred_element_type=F32, precision=HI)


def _ada_kernel(c_ref, w_ref, b_ref, o_ref):
    a = _silu(c_ref[...]).astype(BF16)
    o_ref[...] = _dot(a, w_ref[...].astype(BF16)) + b_ref[...]


def ada_modulation(cs, w_ada, b_ada):
    depth, d, n = w_ada.shape
    tn = 512
    return pl.pallas_call(
        _ada_kernel,
        grid=(depth, n // tn),
        in_specs=[pl.BlockSpec((8, d), lambda l, j: (0, 0)),
                  pl.BlockSpec((None, d, tn), lambda l, j: (l, 0, j)),
                  pl.BlockSpec((None, 1, tn), lambda l, j: (l, 0, j))],
        out_specs=pl.BlockSpec((None, 8, tn), lambda l, j: (l, 0, j)),
        out_shape=jax.ShapeDtypeStruct((depth, 8, n), F32),
        compiler_params=_cp("arbitrary", "arbitrary"),
        name="ada_modulation",
    )(cs, w_ada, b_ada.reshape(depth, 1, n))


def _ln_rows(x):
    xc = x - jnp.mean(x, axis=-1, keepdims=True)
    return xc * lax.rsqrt(jnp.mean(xc * xc, axis=-1, keepdims=True) + EPS)


def _lnmod_kernel(*refs, pre_ln, emit_h, d):
    it = iter(refs)
    s_ref = next(it)
    mod_ref = next(it) if emit_h else None
    g_ref = next(it) if pre_ln else None
    b_ref = next(it) if pre_ln else None
    s_out = next(it) if pre_ln else None
    h_out = next(it) if emit_h else None
    x = s_ref[...]
    if pre_ln:
        x = _ln_rows(x) * g_ref[...] + b_ref[...]
        s_out[...] = x
    if emit_h:
        shift = mod_ref[:, 0:d]
        scale = mod_ref[:, d:2 * d]
        h_out[...] = (_ln_rows(x) * (1.0 + scale) + shift).astype(BF16)


def ln_modulate(s, mod_rows, ln_g, ln_b, n_lat_rows, *, pre_ln, emit_h, rows=None):
    b, _, d = s.shape
    t = s.shape[1] if rows is None else rows
    tr = 256
    nlat = n_lat_rows // tr
    args = [s]
    in_specs = [pl.BlockSpec((None, tr, d), lambda bi, ti: (bi, ti, 0))]
    if emit_h:
        args.append(mod_rows)
        in_specs.append(pl.BlockSpec((None, 1, 3 * d), lambda bi, ti: (jnp.where(ti < nlat, bi, 2), 0, 0)))
    if pre_ln:
        args += [ln_g.reshape(1, d), ln_b.reshape(1, d)]
        in_specs += [pl.BlockSpec((1, d), lambda bi, ti: (0, 0))] * 2
    out_shape, out_specs = [], []
    if pre_ln:
        out_shape.append(jax.ShapeDtypeStruct((b, t, d), F32))
        out_specs.append(pl.BlockSpec((None, tr, d), lambda bi, ti: (bi, ti, 0)))
    if emit_h:
        out_shape.append(jax.ShapeDtypeStruct((b, t, d), BF16))
        out_specs.append(pl.BlockSpec((None, tr, d), lambda bi, ti: (bi, ti, 0)))
    return pl.pallas_call(
        functools.partial(_lnmod_kernel, pre_ln=pre_ln, emit_h=emit_h, d=d),
        grid=(b, t // tr), in_specs=in_specs, out_specs=out_specs, out_shape=out_shape,
        compiler_params=_cp("arbitrary", "arbitrary"), name="ln_modulate",
    )(*args)


def _mm_kernel(a_ref, w_ref, o_ref):
    o_ref[...] = _dot(a_ref[...], w_ref[...]).astype(o_ref.dtype)


def matmul(a, w, out_dtype, name):
    m, k = a.shape
    n = w.shape[1]
    tm = _pick(m, (768, 512, 256))
    tn = _pick(n, (1024, 512, 128))
    return pl.pallas_call(
        _mm_kernel, grid=(m // tm, n // tn),
        in_specs=[pl.BlockSpec((tm, k), lambda i, j: (i, 0)),
                  pl.BlockSpec((k, tn), lambda i, j: (0, j))],
        out_specs=pl.BlockSpec((tm, tn), lambda i, j: (i, j)),
        out_shape=jax.ShapeDtypeStruct((m, n), out_dtype),
        compiler_params=_cp("arbitrary", "arbitrary"), name=name,
    )(a, w)


def _merge_kernel(h_ref, *refs):
    *y_refs, wg_ref, wb_ref, o_ref, acc_ref = refs
    i = pl.program_id(2)
    gate = jax.nn.sigmoid(_dot(h_ref[...], wg_ref[...]))
    for idx, y_ref in enumerate(y_refs):
        @pl.when(i == idx)
        def _(idx=idx, y_ref=y_ref):
            term = gate * _dot(y_ref[...].astype(BF16), wb_ref[...])
            if idx == 0:
                acc_ref[...] = term
            elif idx + 1 < len(y_refs):
                acc_ref[...] += term
            else:
                o_ref[...] = (acc_ref[...] + term).astype(o_ref.dtype)


def gated_merge(h, ys, w_gate, w_br, layer):
    m, d = h.shape
    nb, bw = w_br.shape[1:3]
    tm = _pick(m, (768, 512, 256))
    tn = 512
    return pl.pallas_call(
        _merge_kernel, grid=(m // tm, d // tn, nb),
        in_specs=[pl.BlockSpec((tm, d), lambda i, j, r: (i, 0))]
        + [pl.BlockSpec((tm, bw), lambda i, j, r: (i, 0))] * nb
        + [pl.BlockSpec((None, None, d, tn), lambda i, j, r: (layer, r, 0, j)),
           pl.BlockSpec((None, None, bw, tn), lambda i, j, r: (layer, r, 0, j))],
        out_specs=pl.BlockSpec((tm, tn), lambda i, j, r: (i, j)),
        out_shape=jax.ShapeDtypeStruct((m, d), BF16),
        scratch_shapes=[pltpu.VMEM((tm, tn), F32)],
        compiler_params=_cp("arbitrary", "arbitrary", "arbitrary"), name="gated_merge",
    )(h, *ys, w_gate, w_br)


def _outproj_kernel(m_ref, w_ref, s_ref, gl_ref, gc_ref, o_ref, *, tm, rows_per_batch, n_lat_rows):
    out = _dot(m_ref[...], w_ref[...])
    row = pl.program_id(0) * tm + lax.broadcasted_iota(jnp.int32, (tm, 1), 0)
    is_lat = (row % rows_per_batch) < n_lat_rows
    gate = jnp.where(is_lat, gl_ref[...], gc_ref[...])
    o_ref[...] = ALPHA * s_ref[...] + gate * out


def out_projection(m2, w_out, layer, s2, gate_rows, rows_per_batch, n_lat_rows):
    m, d = m2.shape
    tm = _pick(rows_per_batch, (768, 512, 256))
    tn = 1024
    bpb = rows_per_batch // tm
    return pl.pallas_call(
        functools.partial(_outproj_kernel, tm=tm, rows_per_batch=rows_per_batch, n_lat_rows=n_lat_rows),
        grid=(m // tm, d // tn),
        in_specs=[pl.BlockSpec((tm, d), lambda i, j: (i, 0)),
                  pl.BlockSpec((None, d, tn), lambda i, j: (layer, 0, j)),
                  pl.BlockSpec((tm, tn), lambda i, j: (i, j)),
                  pl.BlockSpec((None, 1, tn), lambda i, j: (i // bpb, 0, j)),
                  pl.BlockSpec((None, 1, tn), lambda i, j: (2, 0, j))],
        out_specs=pl.BlockSpec((tm, tn), lambda i, j: (i, j)),
        out_shape=jax.ShapeDtypeStruct((m, d), F32),
        compiler_params=_cp("arbitrary", "arbitrary"), name="out_projection",
    )(m2, w_out, s2, gate_rows, gate_rows)


CONV_ROWS = 256


def _conv_kernel(x_ref, w_ref, b_ref, o_ref, *, t_rows, n_lat_rows, act):
    r = CONV_ROWS
    w0, w1, w2 = w_ref[0:1, :], w_ref[1:2, :], w_ref[2:3, :]
    bias = b_ref[...]
    rid = lax.broadcasted_iota(jnp.int32, (r, 1), 0)

    def body(i, carry):
        r0 = pl.multiple_of(i * r, r)
        cur = x_ref[pl.ds(r0, r), :].astype(F32)
        p0 = pl.multiple_of(jnp.maximum(r0 - 8, 0), 8)
        n0 = pl.multiple_of(jnp.minimum(r0 + r, t_rows - 8), 8)
        prev_row = x_ref[pl.ds(p0, 8), :].astype(F32)[7:8, :]
        next_row = x_ref[pl.ds(n0, 8), :].astype(F32)[0:1, :]
        up = jnp.where(rid == 0, prev_row, pltpu.roll(cur, 1, 0))
        dn = jnp.where(rid == r - 1, next_row, pltpu.roll(cur, r - 1, 0))
        gpos = r0 + rid
        up = jnp.where((gpos == 0) | (gpos == n_lat_rows), 0.0, up)
        dn = jnp.where((gpos == n_lat_rows - 1) | (gpos == t_rows - 1), 0.0, dn)
        y = w0 * up + w1 * cur + w2 * dn + bias
        if act:
            y = _silu(y)
        o_ref[pl.ds(r0, r), :] = y.astype(o_ref.dtype)
        return carry

    lax.fori_loop(0, t_rows // r, body, 0)


def short_conv(p, col0, width, w, b, n_lat_rows, act, out_dtype, name):
    bsz, t, _ = p.shape
    ct = 256
    cb = col0 // ct
    return pl.pallas_call(
        functools.partial(_conv_kernel, t_rows=t, n_lat_rows=n_lat_rows, act=act),
        grid=(bsz, width // ct),
        in_specs=[pl.BlockSpec((None, t, ct), lambda bi, j: (bi, 0, cb + j)),
                  pl.BlockSpec((3, ct), lambda bi, j: (0, j)),
                  pl.BlockSpec((1, ct), lambda bi, j: (0, j))],
        out_specs=pl.BlockSpec((None, t, ct), lambda bi, j: (bi, 0, j)),
        out_shape=jax.ShapeDtypeStruct((bsz, t, width), out_dtype),
        compiler_params=_cp("arbitrary", "arbitrary"), name=name,
    )(p, w, b.reshape(1, width))


def _filter_kernel(w1_ref, b1_ref, w2_ref, b2_ref, w3_ref, b3_ref, fr_ref, w4_ref, dl_ref, o_ref, *, seq, tr):
    n = pl.program_id(0) * tr + lax.broadcasted_iota(jnp.int32, (tr, 1), 0)
    lag = jnp.where(n < seq, n, 2 * seq - n).astype(F32)
    t = lag * (1.0 / seq)
    lane = lax.broadcasted_iota(jnp.int32, (1, LANE), 1)
    band = jnp.where(lane <= HY_BANDS, lane, lane - HY_BANDS).astype(F32)
    arg = (jnp.float32(2.0 * math.pi) * band) * t
    z = jnp.where(lane == 0, t,
                  jnp.where(lane <= HY_BANDS, jnp.cos(arg),
                            jnp.where(lane <= 2 * HY_BANDS, jnp.sin(arg), 0.0)))
    hdn = jnp.sin(fr_ref[0:1, :] * (_dot_hi(z, w1_ref[...]) + b1_ref[...]))
    hdn = jnp.sin(fr_ref[1:2, :] * (_dot_hi(hdn, w2_ref[...]) + b2_ref[...]))
    hdn = jnp.sin(fr_ref[2:3, :] * (_dot_hi(hdn, w3_ref[...]) + b3_ref[...]))
    filt = _dot_hi(hdn, w4_ref[...])
    win = jnp.exp(-t * dl_ref[...])
    win = jnp.where(n == seq, 0.0, win)
    wdt = o_ref.shape[-1]
    o_ref[0] = filt[:, :wdt] * win
    o_ref[1] = filt[:, wdt:] * win


def hyena_filter(seq, w1p, b1, w2, b2, w3, b3, freq, w4d, deltas_abs):
    tr = 256
    wdt = deltas_abs.shape[-1]
    nblk = 2 * seq // tr
    half = seq // tr
    full = lambda shape: pl.BlockSpec(shape, lambda i: (0,) * len(shape))
    return pl.pallas_call(
        functools.partial(_filter_kernel, seq=seq, tr=tr),
        grid=(nblk,),
        in_specs=[full((LANE, HY_HID)), full((1, HY_HID)), full((HY_HID, HY_HID)), full((1, HY_HID)),
                  full((HY_HID, HY_HID)), full((1, HY_HID)), full((3, HY_HID)),
                  pl.BlockSpec((None, HY_HID, 2 * wdt), lambda i: (jnp.where(i < half, 0, 1), 0, 0)),
                  full((1, wdt))],
        out_specs=pl.BlockSpec((2, tr, wdt), lambda i: (0, i, 0)),
        out_shape=jax.ShapeDtypeStruct((2, 2 * seq, wdt), F32),
        compiler_params=_cp("arbitrary"), name="hyena_filter",
    )(w1p, b1, w2, b2, w3, b3, freq, w4d, deltas_abs)


def _dft_tables(n1):
    n = n1 * DFT_MINOR
    two_pi = 2.0 * math.pi
    ia = jnp.arange(n1, dtype=jnp.int32)
    ib = jnp.arange(DFT_MINOR, dtype=jnp.int32)
    m = (ia[None, :, None] * (DFT_MINOR * ia[None, None, :] + ib[:, None, None])) % n
    ang = m.astype(F32) * (two_pi / n)
    g = jnp.concatenate([jnp.cos(ang), -jnp.sin(ang)], axis=1).astype(BF16)
    th = ((ib[:, None] * ib[None, :]) % DFT_MINOR).astype(F32) * (two_pi / DFT_MINOR)
    c, s = jnp.cos(th), jnp.sin(th)
    mf = jnp.concatenate([jnp.concatenate([c, s], 1), jnp.concatenate([-s, c], 1)], 0).astype(BF16)
    kk = ia[:, None, None] + n1 * ib[None, None, :]
    mi = (ib[None, :, None] * kk) % n
    ps = mi.astype(F32) * (two_pi / n)
    cr, ci = jnp.cos(ps), jnp.sin(ps)
    cinv = jnp.concatenate([jnp.concatenate([cr, -ci], 2), jnp.concatenate([ci, cr], 2)], 1).astype(BF16)
    ph = ((ia[: n1 // 2, None] * ia[None, :]) % n1).astype(F32) * (two_pi / n1)
    m3 = (jnp.concatenate([jnp.cos(ph), -jnp.sin(ph)], 1) * (1.0 / n)).astype(BF16)
    return g, mf, cinv, m3


def _s1_kernel(g_ref, x_ref, o_ref):
    for j in range(SUB):
        o_ref[:, j, :] = _dot(g_ref[j], x_ref[:, j, :].astype(BF16))


def dft_stage1(x4, g, comp, rows, wdt, name):
    gsz = x4.shape[0]
    n1x2 = g.shape[1]
    return pl.pallas_call(
        _s1_kernel, grid=(gsz, DFT_MINOR // SUB),
        in_specs=[pl.BlockSpec((SUB, n1x2, rows), lambda i, bb: (bb, 0, 0)),
                  pl.BlockSpec((None, rows, SUB, wdt), lambda i, bb: (i, 0, bb, comp))],
        out_specs=pl.BlockSpec((None, n1x2, SUB, wdt), lambda i, bb: (i, 0, bb, 0)),
        out_shape=jax.ShapeDtypeStruct((gsz, n1x2, DFT_MINOR, wdt), F32),
        compiler_params=_cp("arbitrary", "arbitrary"), name=name,
    )(g, x4)


def _s2f_kernel(mf_ref, a_ref, o_ref):
    wdt = a_ref.shape[-1]
    for j in range(SUB):
        y = _dot(mf_ref[...], a_ref[:, j].astype(BF16).reshape(2 * DFT_MINOR, wdt))
        o_ref[:, j] = y.reshape(2, DFT_MINOR, wdt)


def filter_spectrum(a5, mf):
    no, _, n1, _, wdt = a5.shape
    ct = 512
    spec = pl.BlockSpec((None, 2, SUB, DFT_MINOR, ct), lambda o, c, jc: (o, 0, c, 0, jc))
    return pl.pallas_call(
        _s2f_kernel, grid=(no, n1 // SUB, wdt // ct),
        in_specs=[pl.BlockSpec((2 * DFT_MINOR, 2 * DFT_MINOR), lambda o, c, jc: (0, 0)), spec],
        out_specs=spec,
        out_shape=jax.ShapeDtypeStruct(a5.shape, F32),
        compiler_params=_cp("arbitrary", "arbitrary", "arbitrary"), name="hyena_filter_spectrum",
    )(mf, a5)


def _s2_kernel(mf_ref, ci_ref, h_ref, a_ref, o_ref):
    wdt = a_ref.shape[-1]
    for j in range(SUB):
        y = _dot(mf_ref[...], a_ref[:, j].astype(BF16).reshape(2 * DFT_MINOR, wdt))
        yr, yi = y[:DFT_MINOR], y[DFT_MINOR:]
        hr, hi = h_ref[0, j], h_ref[1, j]
        z = jnp.concatenate([yr * hr - yi * hi, yr * hi + yi * hr], axis=0).astype(BF16)
        o_ref[:, j, :] = _dot(ci_ref[j], z)


def spectral_multiply(a5, spec5, order, mf, cinv):
    bsz, _, n1, _, wdt = a5.shape
    ct = 512
    return pl.pallas_call(
        _s2_kernel, grid=(n1 // SUB, bsz, wdt // ct),
        in_specs=[pl.BlockSpec((2 * DFT_MINOR, 2 * DFT_MINOR), lambda c, bi, jc: (0, 0)),
                  pl.BlockSpec((SUB, 2 * DFT_MINOR, 2 * DFT_MINOR), lambda c, bi, jc: (c, 0, 0)),
                  pl.BlockSpec((None, 2, SUB, DFT_MINOR, ct), lambda c, bi, jc: (order, 0, c, 0, jc)),
                  pl.BlockSpec((None, 2, SUB, DFT_MINOR, ct), lambda c, bi, jc: (bi, 0, c, 0, jc))],
        out_specs=pl.BlockSpec((None, 2 * DFT_MINOR, SUB, ct), lambda c, bi, jc: (bi, 0, c, jc)),
        out_shape=jax.ShapeDtypeStruct((bsz, 2 * DFT_MINOR, n1, wdt), F32),
        compiler_params=_cp("arbitrary", "arbitrary", "arbitrary"), name="hyena_spectral_multiply",
    )(mf, cinv, spec5, a5)


def _s3_mid_kernel(m3_ref, g_ref, bq_ref, z_ref, x_ref, sk_ref, zo_ref, ao_ref):
    wdt = bq_ref.shape[-1]
    for j in range(SUB):
        y = _dot(m3_ref[...], bq_ref[:, j].astype(BF16).reshape(-1, wdt))
        z = x_ref[:, j, :] * (y + z_ref[:, j, :] * sk_ref[...])
        zo_ref[:, j, :] = z
        ao_ref[:, j, :] = _dot(g_ref[j], z.astype(BF16))


def _s3_last_kernel(m3_ref, bq_ref, z_ref, x_ref, sk_ref, gate_ref, o_ref):
    wdt = bq_ref.shape[-1]
    for j in range(SUB):
        y = _dot(m3_ref[...], bq_ref[:, j].astype(BF16).reshape(-1, wdt))
        z = x_ref[:, j, :] * (y + z_ref[:, j, :] * sk_ref[...])
        o_ref[:, j, :] = z * _silu(gate_ref[:, j, :])


def _hyena_ctx_kernel(ff_ref, fd_ref, fi_ref, circ_ref, v_ref, x1_ref, x2_ref, gate_ref, sk_ref, o_ref, *, nc):
    z = v_ref[...].astype(F32)
    xs = (x1_ref, x2_ref)
    for o in range(2):
        hs = _dot(ff_ref[...], circ_ref[o].astype(BF16))
        us = _dot(fd_ref[...], z.astype(BF16))
        hr, hi = hs[:nc], hs[nc:]
        ur, ui = us[:nc], us[nc:]
        zz = jnp.concatenate([ur * hr - ui * hi, ur * hi + ui * hr], axis=0).astype(BF16)
        y = _dot(fi_ref[...], zz)
        z = xs[o][...].astype(F32) * (y + z * sk_ref[o:o + 1, :])
    o_ref[...] = (z * _silu(gate_ref[...].astype(F32))).astype(o_ref.dtype)


def hyena_context(hv, p, circ_c, skip, n_lat_rows, lc):
    bsz, _, w3 = hv.shape
    wdt = w3 // 3
    nc = 2 * lc
    ct = 256
    two_pi = 2.0 * math.pi
    ik = jnp.arange(nc, dtype=jnp.int32)
    th = ((ik[:, None] * ik[None, :]) % nc).astype(F32) * (two_pi / nc)
    c, s = jnp.cos(th), jnp.sin(th)
    ffull = jnp.concatenate([c, -s], axis=0).astype(BF16)
    fdata = ffull[:, :lc]
    finv = (jnp.concatenate([c[:lc], -s[:lc]], axis=1) * (1.0 / nc)).astype(BF16)
    rb = n_lat_rows // lc
    cw = wdt // ct
    full = lambda shape: pl.BlockSpec(shape, lambda bi, j: (0,) * len(shape))
    return pl.pallas_call(
        functools.partial(_hyena_ctx_kernel, nc=nc),
        grid=(bsz, cw),
        in_specs=[full((2 * nc, nc)), full((2 * nc, lc)), full((lc, 2 * nc)),
                  pl.BlockSpec((2, nc, ct), lambda bi, j: (0, 0, j)),
                  pl.BlockSpec((None, lc, ct), lambda bi, j: (bi, rb, j)),
                  pl.BlockSpec((None, lc, ct), lambda bi, j: (bi, rb, cw + j)),
                  pl.BlockSpec((None, lc, ct), lambda bi, j: (bi, rb, 2 * cw + j)),
                  pl.BlockSpec((None, lc, ct), lambda bi, j: (bi, rb, C_HYGATE // ct + j)),
                  pl.BlockSpec((2, ct), lambda bi, j: (0, j))],
        out_specs=pl.BlockSpec((None, lc, ct), lambda bi, j: (bi, 0, j)),
        out_shape=jax.ShapeDtypeStruct((bsz, lc, wdt), BF16),
        compiler_params=_cp("arbitrary", "arbitrary"), name="hyena_context",
    )(ffull, fdata, finv, circ_c, hv, hv, hv, p, skip)


def hyena_latent(hv, p, circ_l, skip, n_lat_rows, tables):
    g, mf, cinv, m3 = tables
    bsz, t, w3 = hv.shape
    wdt = w3 // 3
    n1 = 2 * n_lat_rows // DFT_MINOR
    half = n1 // 2
    ct = 512
    nj = wdt // ct
    af = dft_stage1(circ_l.reshape(2, n1, DFT_MINOR, wdt), g, 0, n1, wdt, "hyena_filter_stage1")
    spec5 = filter_spectrum(af.reshape(2, 2, n1, DFT_MINOR, wdt), mf)
    gd = g[:, :, :half]
    hv4 = hv.reshape(bsz, t // DFT_MINOR, DFT_MINOR, w3)
    p4 = p.reshape(bsz, t // DFT_MINOR, DFT_MINOR, p.shape[-1])
    sk = skip.reshape(2, 1, wdt)
    a = dft_stage1(hv4, gd, 0, half, wdt, "hyena_stage1")
    bq = spectral_multiply(a.reshape(bsz, 2, n1, DFT_MINOR, wdt), spec5, 0, mf, cinv)
    grid = (bsz, DFT_MINOR // SUB, nj)
    nat = lambda cb: pl.BlockSpec((None, half, SUB, ct), lambda bi, bb, jc: (bi, 0, bb, cb + jc))
    bq_spec = pl.BlockSpec((None, 2, SUB, n1, ct), lambda bi, bb, jc: (bi, 0, bb, 0, jc))
    m3_spec = pl.BlockSpec((half, 2 * n1), lambda bi, bb, jc: (0, 0))
    sk_spec = lambda o: pl.BlockSpec((None, 1, ct), lambda bi, bb, jc: (o, 0, jc))
    z1, a = pl.pallas_call(
        _s3_mid_kernel, grid=grid,
        in_specs=[m3_spec, pl.BlockSpec((SUB, 2 * n1, half), lambda bi, bb, jc: (bb, 0, 0)),
                  bq_spec, nat(0), nat(nj), sk_spec(0)],
        out_specs=[nat(0), pl.BlockSpec((None, 2 * n1, SUB, ct), lambda bi, bb, jc: (bi, 0, bb, jc))],
        out_shape=[jax.ShapeDtypeStruct((bsz, half, DFT_MINOR, wdt), F32),
                   jax.ShapeDtypeStruct((bsz, 2 * n1, DFT_MINOR, wdt), F32)],
        compiler_params=_cp("arbitrary", "arbitrary", "arbitrary"), name="hyena_stage3_mid",
    )(m3, gd, bq.reshape(bsz, 2, DFT_MINOR, n1, wdt), hv4, hv4, sk)
    bq = spectral_multiply(a.reshape(bsz, 2, n1, DFT_MINOR, wdt), spec5, 1, mf, cinv)
    y = pl.pallas_call(
        _s3_last_kernel, grid=grid,
        in_specs=[m3_spec, bq_spec, nat(0), nat(2 * nj), sk_spec(1), nat(C_HYGATE // ct)],
        out_specs=nat(0),
        out_shape=jax.ShapeDtypeStruct((bsz, half, DFT_MINOR, wdt), F32),
        compiler_params=_cp("arbitrary", "arbitrary", "arbitrary"), name="hyena_stage3_last",
    )(m3, bq.reshape(bsz, 2, DFT_MINOR, n1, wdt), z1, hv4, sk, p4)
    return y.reshape(bsz, n_lat_rows, wdt)


SCAN_ROWS = 256


def _scan_block_map(direction, n_lat_rows, t_rows):
    n_lat, n_all = n_lat_rows // SCAN_ROWS, t_rows // SCAN_ROWS
    if direction == 0:
        return lambda t: (t + n_lat) % n_all
    return lambda t: n_all - 1 - t


def _sub_chunks(rows, chunk, direction):
    order = range(rows // chunk) if direction == 0 else reversed(range(rows // chunk))
    return [slice(i * chunk, (i + 1) * chunk) for i in order]


def _tri_mask(n, direction):
    ri = lax.broadcasted_iota(jnp.int32, (n, n), 0)
    ci = lax.broadcasted_iota(jnp.int32, (n, n), 1)
    return (ci <= ri) if direction == 0 else (ci >= ri)


def _expand_heads(v, lane0):
    rows = v.shape[0]
    low_half = lax.broadcasted_iota(jnp.int32, (1, LANE), 1) < SSD_HEADDIM
    tiles = []
    for k in range(SSD_HEADS // 2):
        lo = jnp.broadcast_to(v[:, lane0 + 2 * k:lane0 + 2 * k + 1], (rows, LANE))
        hi = jnp.broadcast_to(v[:, lane0 + 2 * k + 1:lane0 + 2 * k + 2], (rows, LANE))
        tiles.append(jnp.where(low_half, lo, hi))
    return jnp.concatenate(tiles, axis=1)


def _ssd_kernel(*refs, direction):
    d = direction
    if d == 0:
        xs_ref, bm_ref, cm_ref, sm_ref, dtb_ref, alog_ref, o_ref, st_ref = refs
    else:
        (xs_ref, bm_ref, cm_ref, sm_ref, dtb_ref, alog_ref,
         y0_ref, z_ref, dsk_ref, nw_ref, o_ref, st_ref) = refs
    q = SSD_CHUNK
    hd = SSD_HEADDIM
    gw = SSD_RPG * hd

    @pl.when(pl.program_id(1) == 0)
    def _():
        st_ref[...] = jnp.zeros_like(st_ref)

    tri = _tri_mask(q, d)
    trif = tri.astype(F32)
    neg_a = -jnp.exp(alog_ref[...])
    for rs in _sub_chunks(xs_ref.shape[0], q, d):
        xs = xs_ref[rs, :].astype(F32)
        bm = bm_ref[rs, :].astype(BF16)
        cm = cm_ref[rs, :].astype(BF16)
        dt = _softplus(sm_ref[rs, :] + dtb_ref[...])
        a = dt * neg_a
        acum = _dot_hi(trif, a)
        acum_t = acum.T
        dtx = _expand_heads(dt, SSD_HEADS * d)
        ax = _expand_heads(acum, SSD_HEADS * d)
        atx = ax[q - 1:q, :] if d == 0 else ax[0:1, :]
        xdt = xs * dtx
        e_in = jnp.exp(ax)
        xw = (xdt * jnp.exp(atx - ax)).astype(BF16)
        xdt_b = xdt.astype(BF16)
        ys = []
        for g in range(SSD_GROUPS):
            bg = bm[:, g * SSD_STATE:(g + 1) * SSD_STATE]
            cg = cm[:, g * SSD_STATE:(g + 1) * SSD_STATE]
            sc = _nt(cg, bg)
            st_g = st_ref[g * gw:(g + 1) * gw, :]
            y_int = _nt(cg, st_g.astype(BF16))
            yh, decs = [], []
            for r in range(SSD_RPG):
                hl = SSD_HEADS * d + SSD_RPG * g + r
                seg = acum[:, hl:hl + 1] - acum_t[hl:hl + 1, :]
                dm = jnp.exp(jnp.where(tri, seg, -1e30))
                ch = (SSD_RPG * g + r) * hd
                yh.append(_dot((sc * dm).astype(BF16), xdt_b[:, ch:ch + hd]))
                a_tot = acum[q - 1:q, hl:hl + 1] if d == 0 else acum[0:1, hl:hl + 1]
                decs.append(jnp.broadcast_to(jnp.exp(a_tot), (hd, SSD_STATE)))
            ys.append(jnp.concatenate(yh, axis=1) + y_int * e_in[:, g * gw:(g + 1) * gw])
            st_ref[g * gw:(g + 1) * gw, :] = (st_g * jnp.concatenate(decs, axis=0)
                                              + _tn(xw[:, g * gw:(g + 1) * gw], bg))
        y = jnp.concatenate(ys, axis=1)
        if d == 0:
            o_ref[rs, :] = y
        else:
            y = (y0_ref[rs, :] + y + dsk_ref[...] * xs) * _silu(z_ref[rs, :].astype(F32))
            parts = []
            for g in range(SSD_GROUPS):
                yg = y[:, g * gw:(g + 1) * gw]
                parts.append(yg * lax.rsqrt(jnp.mean(yg * yg, axis=-1, keepdims=True) + EPS))
            o_ref[rs, :] = (jnp.concatenate(parts, axis=1) * nw_ref[...]).astype(o_ref.dtype)


def ssd_mixer(xbc, p, psmall, dtb, alog, dskip, normw, n_lat_rows):
    bsz, t, _ = xbc.shape
    q = SCAN_ROWS
    w = BRANCH_W
    nct = t // q
    outs = None
    for d in range(2):
        cm_ = _scan_block_map(d, n_lat_rows, t)
        row = lambda width, cb, cm_=cm_: pl.BlockSpec((None, q, width), lambda bi, ti: (bi, cm_(ti), cb))
        const = lambda shape: pl.BlockSpec(shape, lambda bi, ti: (0,) * len(shape))
        in_specs = [row(w, 0), row(4 * SSD_STATE, 2), row(4 * SSD_STATE, 3), row(NSMALL, 0),
                    const((1, NSMALL)), const((1, NSMALL))]
        args = [xbc, xbc, xbc, psmall, dtb, alog]
        if d == 1:
            in_specs += [row(w, 0), row(w, C_SSDZ // w), const((1, w)), const((1, w))]
            args += [outs, p, dskip, normw]
        outs = pl.pallas_call(
            functools.partial(_ssd_kernel, direction=d), grid=(bsz, nct),
            in_specs=in_specs, out_specs=row(w, 0),
            out_shape=jax.ShapeDtypeStruct((bsz, t, w), F32 if d == 0 else BF16),
            scratch_shapes=[pltpu.VMEM((SSD_HEADS * SSD_HEADDIM, SSD_STATE), F32)],
            compiler_params=_cp("arbitrary", "arbitrary"), name=f"ssd_scan_dir{d}",
        )(*args)
    return outs


def _gla_kernel(*refs, direction):
    d = direction
    if d == 0:
        q_ref, k_ref, v_ref, sm_ref, w2_ref, b2_ref, o_ref, st_ref = refs
    else:
        q_ref, k_ref, v_ref, sm_ref, w2_ref, b2_ref, y0_ref, g_ref, nw_ref, o_ref, st_ref = refs
    cs = GLA_CHUNK
    dk, dv = GLA_HDK, GLA_HDV

    @pl.when(pl.program_id(1) == 0)
    def _():
        st_ref[...] = jnp.zeros_like(st_ref)

    tri = _tri_mask(cs, d)
    trif = tri.astype(F32)
    mid = cs // 2 if d == 0 else cs - 1 - cs // 2
    for rs in _sub_chunks(q_ref.shape[0], cs, d):
        logit = _dot(sm_ref[rs, :].astype(BF16), w2_ref[...]) + b2_ref[...]
        gl = _log_sigmoid(logit) * (1.0 / GLA_NORMALIZER)
        gc = _dot_hi(trif, gl)
        g_mid = gc[mid:mid + 1, :]
        g_last = gc[cs - 1:cs, :] if d == 0 else gc[0:1, :]
        qf = q_ref[rs, :].astype(F32) * (dk ** -0.5)
        kf = k_ref[rs, :].astype(F32)
        vb = v_ref[rs, :].astype(BF16)
        qa = (qf * jnp.exp(gc - g_mid)).astype(BF16)
        ka = (kf * jnp.exp(g_mid - gc)).astype(BF16)
        qs = (qf * jnp.exp(gc)).astype(BF16)
        ke = (kf * jnp.exp(g_last - gc)).astype(BF16)
        dec = jnp.exp(g_last)
        ys = []
        for h in range(GLA_HEADS):
            ks, vs = slice(h * dk, (h + 1) * dk), slice(h * dv, (h + 1) * dv)
            att = jnp.where(tri, _nt(qa[:, ks], ka[:, ks]), 0.0)
            st_h = st_ref[h]
            ys.append(_dot(att.astype(BF16), vb[:, vs]) + _nt(qs[:, ks], st_h.astype(BF16)))
            st_ref[h] = st_h * dec[:, ks] + _tn(vb[:, vs], ke[:, ks])
        y = jnp.concatenate(ys, axis=1)
        if d == 0:
            o_ref[rs, :] = y
        else:
            y = y0_ref[rs, :] + y
            parts = []
            for h in range(GLA_HEADS):
                yh = y[:, h * dv:(h + 1) * dv]
                parts.append(yh * lax.rsqrt(jnp.mean(yh * yh, axis=-1, keepdims=True) + EPS))
            y = jnp.concatenate(parts, axis=1) * nw_ref[...]
            o_ref[rs, :] = (y * _silu(g_ref[rs, :].astype(F32))).astype(o_ref.dtype)


def gla_mixer(p, psmall, w2p, b2, normw, n_lat_rows):
    bsz, t, _ = p.shape
    cs = SCAN_ROWS
    w = BRANCH_W
    hw = GLA_HEADS * GLA_HDK
    nct = t // cs
    outs = None
    for d in range(2):
        cm_ = _scan_block_map(d, n_lat_rows, t)
        row = lambda width, cb, cm_=cm_: pl.BlockSpec((None, cs, width), lambda bi, ti: (bi, cm_(ti), cb))
        const = lambda shape: pl.BlockSpec(shape, lambda bi, ti: (0,) * len(shape))
        in_specs = [row(hw, C_GQ // hw), row(hw, C_GK // hw), row(w, C_GV // w), row(NSMALL, 0),
                    pl.BlockSpec((None, NSMALL, hw), lambda bi, ti, d=d: (d, 0, 0)),
                    pl.BlockSpec((None, 1, hw), lambda bi, ti, d=d: (d, 0, 0))]
        args = [p, p, p, psmall, w2p, b2]
        if d == 1:
            in_specs += [row(w, 0), row(w, C_GG // w), const((1, w))]
            args += [outs, p, normw]
        outs = pl.pallas_call(
            functools.partial(_gla_kernel, direction=d), grid=(bsz, nct),
            in_specs=in_specs, out_specs=row(w, 0),
            out_shape=jax.ShapeDtypeStruct((bsz, t, w), F32 if d == 0 else BF16),
            scratch_shapes=[pltpu.VMEM((GLA_HEADS, GLA_HDV, GLA_HDK), F32)],
            compiler_params=_cp("arbitrary", "arbitrary"), name=f"gla_scan_dir{d}",
        )(*args)
    return outs


def _rope(x, cos, sin_signed):
    half = RET_HD // 2
    lo, hi = x[:, :half], x[:, half:]
    lo = lo * cos[:, :half] + pltpu.roll(lo, half // 2, 1) * sin_signed[:, :half]
    hi = hi * cos[:, half:] + pltpu.roll(hi, half // 2, 1) * sin_signed[:, half:]
    return jnp.concatenate([lo, hi], axis=1)


def _ret_kernel(*refs, direction):
    d = direction
    if d == 0:
        q_ref, k_ref, v_ref, cos_ref, sin_ref, dr_ref, o_ref, st_ref = refs
    else:
        q_ref, k_ref, v_ref, cos_ref, sin_ref, dr_ref, y0_ref, g_ref, o_ref, st_ref = refs
    cs = RET_CHUNK
    hdim = RET_HD

    @pl.when(pl.program_id(1) == 0)
    def _():
        st_ref[...] = jnp.zeros_like(st_ref)

    lam_all = -jnp.exp(dr_ref[...])
    tri = _tri_mask(cs, d)
    ri = lax.broadcasted_iota(jnp.int32, (cs, cs), 0)
    ci = lax.broadcasted_iota(jnp.int32, (cs, cs), 1)
    lag = jnp.abs(ri - ci).astype(F32)
    pos = lax.broadcasted_iota(jnp.int32, (cs, hdim), 0).astype(F32)
    steps_in = (pos + 1.0) if d == 0 else (cs - pos)
    steps_out = (cs - 1.0 - pos) if d == 0 else pos
    lams = [lam_all[d:d + 1, h:h + 1] for h in range(RET_HEADS)]
    dms = [jnp.where(tri, jnp.exp(lam * lag), 0.0) for lam in lams]
    w_in = [jnp.exp(lam * steps_in) for lam in lams]
    w_out = [jnp.exp(lam * steps_out) for lam in lams]
    w_chunk = [jnp.exp(lam * cs) for lam in lams]
    for rs in _sub_chunks(q_ref.shape[0], cs, d):
        cos, sin_s = cos_ref[rs, :], sin_ref[rs, :]
        qf = q_ref[rs, :].astype(F32)
        kf = k_ref[rs, :].astype(F32) * (hdim ** -0.5)
        vb = v_ref[rs, :].astype(BF16)
        ys = []
        for h in range(RET_HEADS):
            hs = slice(h * hdim, (h + 1) * hdim)
            qh = _rope(qf[:, hs], cos, sin_s)
            kh = _rope(kf[:, hs], cos, sin_s)
            qb = qh.astype(BF16)
            att = (_nt(qb, kh.astype(BF16)) * dms[h]).astype(BF16)
            st_h = st_ref[h]
            ys.append(_dot(att, vb[:, hs]) + _dot(qb, st_h.astype(BF16)) * w_in[h])
            st_ref[h] = st_h * w_chunk[h] + _tn((kh * w_out[h]).astype(BF16), vb[:, hs])
        y = jnp.concatenate(ys, axis=1)
        if d == 0:
            o_ref[rs, :] = y
        else:
            y = y0_ref[rs, :] + y
            y = jnp.concatenate([_ln_rows(y[:, h * hdim:(h + 1) * hdim]) for h in range(RET_HEADS)], axis=1)
            o_ref[rs, :] = (y * _silu(g_ref[rs, :].astype(F32))).astype(o_ref.dtype)


def ret_mixer(p, cos_t, sin_t, decay_pad, n_lat_rows):
    bsz, t, _ = p.shape
    cs = SCAN_ROWS
    w = BRANCH_W
    nct = t // cs
    outs = None
    for d in range(2):
        cm_ = _scan_block_map(d, n_lat_rows, t)
        row = lambda cb, cm_=cm_: pl.BlockSpec((None, cs, w), lambda bi, ti: (bi, cm_(ti), cb))
        tab = pl.BlockSpec((cs, RET_HD), lambda bi, ti, cm_=cm_: (cm_(ti), 0))
        in_specs = [row(C_RQ // w), row(C_RK // w), row(C_RV // w), tab, tab,
                    pl.BlockSpec((8, LANE), lambda bi, ti: (0, 0))]
        args = [p, p, p, cos_t, sin_t, decay_pad]
        if d == 1:
            in_specs += [row(0), row(C_RG // w)]
            args += [outs, p]
        outs = pl.pallas_call(
            functools.partial(_ret_kernel, direction=d), grid=(bsz, nct),
            in_specs=in_specs, out_specs=row(0),
            out_shape=jax.ShapeDtypeStruct((bsz, t, w), F32 if d == 0 else BF16),
            scratch_shapes=[pltpu.VMEM((RET_HEADS, RET_HD, RET_HD), F32)],
            compiler_params=_cp("arbitrary", "arbitrary"), name=f"ret_scan_dir{d}",
        )(*args)
    return outs


def _rope_tables(n_lat_rows, lc):
    half = RET_HD // 2
    inv = ROPE_BASE ** (-jnp.arange(0, half, 2, dtype=F32) / half)
    tpos = jnp.arange(n_lat_rows)
    row = (tpos // GRID_W).astype(F32)[:, None] * inv[None, :]
    col = (tpos % GRID_W).astype(F32)[:, None] * inv[None, :]
    cos = jnp.concatenate([jnp.cos(row)] * 2 + [jnp.cos(col)] * 2, axis=1)
    sin = jnp.concatenate([-jnp.sin(row), jnp.sin(row), -jnp.sin(col), jnp.sin(col)], axis=1)
    cos = jnp.concatenate([cos, jnp.ones((lc, RET_HD), F32)], axis=0)
    sin = jnp.concatenate([sin, jnp.zeros((lc, RET_HD), F32)], axis=0)
    return cos, sin


def _pad_lanes(v, start):
    v = v.reshape(-1).astype(F32)
    return jnp.zeros((1, NSMALL), F32).at[0, start:start + v.shape[0]].set(v)


def kernel(x, c, ctx, c_ctx, w_ada, b_ada, w_in, hy_conv_w, hy_conv_b, hy_w1, hy_b1, hy_w2, hy_b2, hy_w3,
           hy_b3, hy_w4, hy_freq, hy_skip, ssd_conv_w, ssd_conv_b, ssd_a_log, ssd_dt_bias, ssd_d, ssd_norm_w,
           gla_w2, gla_b2, gla_norm_w, ret_decay, w_gate, w_br, w_out, ln_g, ln_b):
    bsz, n_lat, d = x.shape
    lc = ctx.shape[1]
    t = n_lat + lc
    depth = w_in.shape[0]
    w = BRANCH_W
    assert bsz <= 2 and d == D_MODEL and n_lat % 256 == 0 and lc % 256 == 0

    s = jnp.concatenate([x, ctx], axis=1)
    cs = jnp.zeros((8, d), F32).at[:bsz].set(c).at[2].set(c_ctx)
    mod = ada_modulation(cs, w_ada, b_ada).reshape(depth, 8, 1, 3 * d)

    cos_t, sin_t = _rope_tables(n_lat, lc)
    tables = _dft_tables(2 * n_lat // DFT_MINOR)
    deltas = jnp.abs(jnp.linspace(math.log(HY_TARGET) / HY_SLOW, math.log(HY_TARGET) / HY_FAST, w,
                                  dtype=F32)).reshape(1, w)

    w_gate_b, w_br_b, w_out_b = w_gate.astype(BF16), w_br.astype(BF16), w_out.astype(BF16)
    h = ln_modulate(s, mod[0], None, None, n_lat, pre_ln=False, emit_h=True)[0]
    for l in range(depth):
        wi = w_in[l]
        w_main = jnp.concatenate([wi[:, 0:6144], wi[:, 6176:9248], wi[:, 9280:14400]], axis=1).astype(BF16)
        w_small = jnp.concatenate([wi[:, 6144:6176], wi[:, 9248:9280], jnp.zeros((d, NSMALL - 64), F32)],
                                  axis=1).astype(BF16)
        h2 = h.reshape(bsz * t, d)
        p = matmul(h2, w_main, F32, "in_projection").reshape(bsz, t, NP)
        psmall = matmul(h2, w_small, F32, "in_projection_small").reshape(bsz, t, NSMALL)

        hv = short_conv(p, C_HYIN, 3 * w, hy_conv_w[l], hy_conv_b[l], n_lat, False, F32, "hyena_short_conv")
        w1p = jnp.zeros((LANE, HY_HID), F32).at[:2 * HY_BANDS + 1].set(hy_w1[l])
        w4d = hy_w4[l].reshape(HY_HID, 2, 2, w).transpose(2, 0, 1, 3).reshape(2, HY_HID, 2 * w)
        fargs = (w1p, hy_b1[l].reshape(1, -1), hy_w2[l], hy_b2[l].reshape(1, -1), hy_w3[l],
                 hy_b3[l].reshape(1, -1), hy_freq[l], w4d, deltas)
        circ_l = hyena_filter(n_lat, *fargs)
        circ_c = hyena_filter(lc, *fargs)
        y_hy = jnp.concatenate([hyena_latent(hv, p, circ_l, hy_skip[l], n_lat, tables).astype(BF16),
                                hyena_context(hv, p, circ_c, hy_skip[l], n_lat, lc)], axis=1)

        xbc = short_conv(p, C_XBC, 2 * w, ssd_conv_w[l], ssd_conv_b[l], n_lat, True, F32, "ssd_short_conv")
        y_ssd = ssd_mixer(xbc, p, psmall, _pad_lanes(ssd_dt_bias[l], 0), _pad_lanes(ssd_a_log[l], 0),
                          jnp.repeat(ssd_d[l].astype(F32), SSD_HEADDIM).reshape(1, w),
                          ssd_norm_w[l].astype(F32).reshape(1, w), n_lat)

        hw = GLA_HEADS * GLA_HDK
        w2p = jnp.zeros((2, NSMALL, hw), F32)
        for dd in range(2):
            w2p = w2p.at[dd, 32 + GLA_RANK * dd:32 + GLA_RANK * (dd + 1)].set(gla_w2[l, dd])
        y_gla = gla_mixer(p, psmall, w2p.astype(BF16), gla_b2[l].astype(F32).reshape(2, 1, hw),
                          jnp.tile(gla_norm_w[l].astype(F32), GLA_HEADS).reshape(1, w), n_lat)

        decay_pad = jnp.zeros((8, LANE), F32).at[:2, :RET_HEADS].set(ret_decay[l])
        y_ret = ret_mixer(p, cos_t, sin_t, decay_pad, n_lat)

        ys = [y.reshape(bsz * t, w) for y in (y_hy, y_ssd, y_gla, y_ret)]
        m = gated_merge(h2, ys, w_gate_b, w_br_b, l)
        gate_rows = mod[l][:, :, 2 * d:3 * d]
        pre = out_projection(m, w_out_b, l, s.reshape(bsz * t, d), gate_rows, t, n_lat)
        pre = pre.reshape(bsz, t, d)
        if l + 1 < depth:
            s, h = ln_modulate(pre, mod[l + 1], ln_g[l], ln_b[l], n_lat, pre_ln=True, emit_h=True)
        else:
            s = ln_modulate(pre, None, ln_g[l], ln_b[l], n_lat, pre_ln=True, emit_h=False, rows=n_lat)[0]
    return s
```

```python
import functools
import math

import jax
import jax.numpy as jnp
from jax import lax
from jax.experimental import pallas as pl
from jax.experimental.pallas import tpu as pltpu

F32 = jnp.float32
BF16 = jnp.bfloat16
HI = lax.Precision.HIGHEST

D_MODEL = 4096
DEPTH = 2
BRANCH_W = 1024
GRID_W = 64
EPS = 1e-6
ALPHA = (2 * DEPTH) ** 0.25

HY_BANDS = 16
HY_HID = 64
HY_TARGET = 1e-2
HY_FAST = 0.3
HY_SLOW = 1.5

SSD_HEADS = 16
SSD_HEADDIM = 64
SSD_GROUPS = 4
SSD_RPG = 4
SSD_STATE = 128
SSD_CHUNK = 128

GLA_HEADS = 4
GLA_HDK = 128
GLA_HDV = 256
GLA_RANK = 16
GLA_NORMALIZER = 16.0
GLA_CHUNK = 64

RET_HEADS = 4
RET_HD = 256
RET_CHUNK = 128
ROPE_BASE = 10000.0

NP = 14336
C_HYIN, C_HYGATE, C_XBC, C_SSDZ = 0, 3072, 4096, 6144
C_GQ, C_GK, C_GV, C_GG = 7168, 7680, 8192, 9216
C_RQ, C_RK, C_RV, C_RG = 10240, 11264, 12288, 13312
NSMALL = 128
LANE = 128
DFT_MINOR = 128
SUB = 8

VMEM_LIMIT = 52 * 1024 * 1024


def _cp(*sem):
    return pltpu.CompilerParams(dimension_semantics=sem, vmem_limit_bytes=VMEM_LIMIT)


def _pick(n, cands):
    for c in cands:
        if n % c == 0:
            return c
    raise ValueError(f"no tile for {n} in {cands}")


def _silu(x):
    return x * jax.nn.sigmoid(x)


def _softplus(x):
    return jnp.maximum(x, 0.0) + jnp.log1p(jnp.exp(-jnp.abs(x)))


def _log_sigmoid(x):
    return jnp.minimum(x, 0.0) - jnp.log1p(jnp.exp(-jnp.abs(x)))


def _nt(a, b):
    return lax.dot_general(a, b, (((1,), (1,)), ((), ())), preferred_element_type=F32)


def _tn(a, b):
    return lax.dot_general(a, b, (((0,), (0,)), ((), ())), preferred_element_type=F32)


def _dot(a, b):
    return jnp.dot(a, b, preferred_element_type=F32)


def _dot_hi(a, b):
    return jnp.dot(a, b, preferred_element_type=F32, precision=HI)


def _ada_kernel(c_ref, w_ref, b_ref, o_ref):
    a = _silu(c_ref[...]).astype(BF16)
    o_ref[...] = _dot(a, w_ref[...].astype(BF16)) + b_ref[...]


def ada_modulation(cs, w_ada, b_ada):
    depth, d, n = w_ada.shape
    tn = 512
    return pl.pallas_call(
        _ada_kernel,
        grid=(depth, n // tn),
        in_specs=[pl.BlockSpec((8, d), lambda l, j: (0, 0)),
                  pl.BlockSpec((None, d, tn), lambda l, j: (l, 0, j)),
                  pl.BlockSpec((None, 1, tn), lambda l, j: (l, 0, j))],
        out_specs=pl.BlockSpec((None, 8, tn), lambda l, j: (l, 0, j)),
        out_shape=jax.ShapeDtypeStruct((depth, 8, n), F32),
        compiler_params=_cp("arbitrary", "arbitrary"),
        name="ada_modulation",
    )(cs, w_ada, b_ada.reshape(depth, 1, n))


def _ln_rows(x):
    xc = x - jnp.mean(x, axis=-1, keepdims=True)
    return xc * lax.rsqrt(jnp.mean(xc * xc, axis=-1, keepdims=True) + EPS)


def _lnmod_kernel(*refs, pre_ln, emit_h, d):
    it = iter(refs)
    s_ref = next(it)
    mod_ref = next(it) if emit_h else None
    g_ref = next(it) if pre_ln else None
    b_ref = next(it) if pre_ln else None
    s_out = next(it) if pre_ln else None
    h_out = next(it) if emit_h else None
    x = s_ref[...]
    if pre_ln:
        x = _ln_rows(x) * g_ref[...] + b_ref[...]
        s_out[...] = x
    if emit_h:
        shift = mod_ref[:, 0:d]
        scale = mod_ref[:, d:2 * d]
        h_out[...] = (_ln_rows(x) * (1.0 + scale) + shift).astype(BF16)


def ln_modulate(s, mod_rows, ln_g, ln_b, n_lat_rows, *, pre_ln, emit_h, rows=None):
    b, _, d = s.shape
    t = s.shape[1] if rows is None else rows
    tr = 256
    nlat = n_lat_rows // tr
    args = [s]
    in_specs = [pl.BlockSpec((None, tr, d), lambda bi, ti: (bi, ti, 0))]
    if emit_h:
        args.append(mod_rows)
        in_specs.append(pl.BlockSpec((None, 1, 3 * d), lambda bi, ti: (jnp.where(ti < nlat, bi, 2), 0, 0)))
    if pre_ln:
        args += [ln_g.reshape(1, d), ln_b.reshape(1, d)]
        in_specs += [pl.BlockSpec((1, d), lambda bi, ti: (0, 0))] * 2
    out_shape, out_specs = [], []
    if pre_ln:
        out_shape.append(jax.ShapeDtypeStruct((b, t, d), F32))
        out_specs.append(pl.BlockSpec((None, tr, d), lambda bi, ti: (bi, ti, 0)))
    if emit_h:
        out_shape.append(jax.ShapeDtypeStruct((b, t, d), BF16))
        out_specs.append(pl.BlockSpec((None, tr, d), lambda bi, ti: (bi, ti, 0)))
    return pl.pallas_call(
        functools.partial(_lnmod_kernel, pre_ln=pre_ln, emit_h=emit_h, d=d),
        grid=(b, t // tr), in_specs=in_specs, out_specs=out_specs, out_shape=out_shape,
        compiler_params=_cp("arbitrary", "arbitrary"), name="ln_modulate",
    )(*args)


def _mm_kernel(a_ref, w_ref, o_ref):
    o_ref[...] = _dot(a_ref[...], w_ref[...]).astype(o_ref.dtype)


def matmul(a, w, out_dtype, name):
    m, k = a.shape
    n = w.shape[1]
    tm = _pick(m, (768, 512, 256))
    tn = _pick(n, (1024, 512, 128))
    return pl.pallas_call(
        _mm_kernel, grid=(m // tm, n // tn),
        in_specs=[pl.BlockSpec((tm, k), lambda i, j: (i, 0)),
                  pl.BlockSpec((k, tn), lambda i, j: (0, j))],
        out_specs=pl.BlockSpec((tm, tn), lambda i, j: (i, j)),
        out_shape=jax.ShapeDtypeStruct((m, n), out_dtype),
        compiler_params=_cp("arbitrary", "arbitrary"), name=name,
    )(a, w)


def _merge_kernel(h_ref, y_ref, wg_ref, wb_ref, o_ref, acc_ref):
    i = pl.program_id(2)
    term = jax.nn.sigmoid(_dot(h_ref[...], wg_ref[...])) * _dot(y_ref[...], wb_ref[...])

    @pl.when(i == 0)
    def _():
        acc_ref[...] = term

    @pl.when(i > 0)
    def _():
        acc_ref[...] += term

    @pl.when(i == pl.num_programs(2) - 1)
    def _():
        o_ref[...] = acc_ref[...].astype(o_ref.dtype)


def gated_merge(h, y, w_gate, w_br, layer):
    m, d = h.shape
    nb, bw = w_br.shape[1:3]
    tm = _pick(m, (512, 256))
    tn = 1024
    return pl.pallas_call(
        _merge_kernel, grid=(m // tm, d // tn, nb),
        in_specs=[pl.BlockSpec((tm, d), lambda i, j, r: (i, 0)),
                  pl.BlockSpec((tm, bw), lambda i, j, r: (i, r)),
                  pl.BlockSpec((None, None, d, tn), lambda i, j, r: (layer, r, 0, j)),
                  pl.BlockSpec((None, None, bw, tn), lambda i, j, r: (layer, r, 0, j))],
        out_specs=pl.BlockSpec((tm, tn), lambda i, j, r: (i, j)),
        out_shape=jax.ShapeDtypeStruct((m, d), BF16),
        scratch_shapes=[pltpu.VMEM((tm, tn), F32)],
        compiler_params=_cp("arbitrary", "arbitrary", "arbitrary"), name="gated_merge",
    )(h, y, w_gate, w_br)


def _outproj_kernel(m_ref, w_ref, s_ref, gl_ref, gc_ref, o_ref, *, tm, rows_per_batch, n_lat_rows):
    out = _dot(m_ref[...], w_ref[...])
    row = pl.program_id(0) * tm + lax.broadcasted_iota(jnp.int32, (tm, 1), 0)
    is_lat = (row % rows_per_batch) < n_lat_rows
    gate = jnp.where(is_lat, gl_ref[...], gc_ref[...])
    o_ref[...] = ALPHA * s_ref[...] + gate * out


def out_projection(m2, w_out, layer, s2, gate_rows, rows_per_batch, n_lat_rows):
    m, d = m2.shape
    tm = _pick(rows_per_batch, (768, 512, 256))
    tn = 1024
    bpb = rows_per_batch // tm
    return pl.pallas_call(
        functools.partial(_outproj_kernel, tm=tm, rows_per_batch=rows_per_batch, n_lat_rows=n_lat_rows),
        grid=(m // tm, d // tn),
        in_specs=[pl.BlockSpec((tm, d), lambda i, j: (i, 0)),
                  pl.BlockSpec((None, d, tn), lambda i, j: (layer, 0, j)),
                  pl.BlockSpec((tm, tn), lambda i, j: (i, j)),
                  pl.BlockSpec((None, 1, tn), lambda i, j: (i // bpb, 0, j)),
                  pl.BlockSpec((None, 1, tn), lambda i, j: (2, 0, j))],
        out_specs=pl.BlockSpec((tm, tn), lambda i, j: (i, j)),
        out_shape=jax.ShapeDtypeStruct((m, d), F32),
        compiler_params=_cp("arbitrary", "arbitrary"), name="out_projection",
    )(m2, w_out, s2, gate_rows, gate_rows)


CONV_ROWS = 256


def _conv_kernel(x_ref, w_ref, b_ref, o_ref, *, t_rows, n_lat_rows, act):
    r = CONV_ROWS
    w0, w1, w2 = w_ref[0:1, :], w_ref[1:2, :], w_ref[2:3, :]
    bias = b_ref[...]
    rid = lax.broadcasted_iota(jnp.int32, (r, 1), 0)

    def body(i, carry):
        r0 = pl.multiple_of(i * r, r)
        cur = x_ref[pl.ds(r0, r), :].astype(F32)
        p0 = pl.multiple_of(jnp.maximum(r0 - 8, 0), 8)
        n0 = pl.multiple_of(jnp.minimum(r0 + r, t_rows - 8), 8)
        prev_row = x_ref[pl.ds(p0, 8), :].astype(F32)[7:8, :]
        next_row = x_ref[pl.ds(n0, 8), :].astype(F32)[0:1, :]
        up = jnp.where(rid == 0, prev_row, pltpu.roll(cur, 1, 0))
        dn = jnp.where(rid == r - 1, next_row, pltpu.roll(cur, r - 1, 0))
        gpos = r0 + rid
        up = jnp.where((gpos == 0) | (gpos == n_lat_rows), 0.0, up)
        dn = jnp.where((gpos == n_lat_rows - 1) | (gpos == t_rows - 1), 0.0, dn)
        y = w0 * up + w1 * cur + w2 * dn + bias
        if act:
            y = _silu(y)
        o_ref[pl.ds(r0, r), :] = y.astype(o_ref.dtype)
        return carry

    lax.fori_loop(0, t_rows // r, body, 0)


def short_conv(p, col0, width, w, b, n_lat_rows, act, out_dtype, name):
    bsz, t, _ = p.shape
    ct = 256
    cb = col0 // ct
    return pl.pallas_call(
        functools.partial(_conv_kernel, t_rows=t, n_lat_rows=n_lat_rows, act=act),
        grid=(bsz, width // ct),
        in_specs=[pl.BlockSpec((None, t, ct), lambda bi, j: (bi, 0, cb + j)),
                  pl.BlockSpec((3, ct), lambda bi, j: (0, j)),
                  pl.BlockSpec((1, ct), lambda bi, j: (0, j))],
        out_specs=pl.BlockSpec((None, t, ct), lambda bi, j: (bi, 0, j)),
        out_shape=jax.ShapeDtypeStruct((bsz, t, width), out_dtype),
        compiler_params=_cp("arbitrary", "arbitrary"), name=name,
    )(p, w, b.reshape(1, width))


def _filter_kernel(w1_ref, b1_ref, w2_ref, b2_ref, w3_ref, b3_ref, fr_ref, w4_ref, dl_ref, o_ref, *, seq, tr):
    n = pl.program_id(0) * tr + lax.broadcasted_iota(jnp.int32, (tr, 1), 0)
    lag = jnp.where(n < seq, n, 2 * seq - n).astype(F32)
    t = lag * (1.0 / seq)
    lane = lax.broadcasted_iota(jnp.int32, (1, LANE), 1)
    band = jnp.where(lane <= HY_BANDS, lane, lane - HY_BANDS).astype(F32)
    arg = (jnp.float32(2.0 * math.pi) * band) * t
    z = jnp.where(lane == 0, t,
                  jnp.where(lane <= HY_BANDS, jnp.cos(arg),
                            jnp.where(lane <= 2 * HY_BANDS, jnp.sin(arg), 0.0)))
    hdn = jnp.sin(fr_ref[0:1, :] * (_dot_hi(z, w1_ref[...]) + b1_ref[...]))
    hdn = jnp.sin(fr_ref[1:2, :] * (_dot_hi(hdn, w2_ref[...]) + b2_ref[...]))
    hdn = jnp.sin(fr_ref[2:3, :] * (_dot_hi(hdn, w3_ref[...]) + b3_ref[...]))
    filt = _dot_hi(hdn, w4_ref[...])
    win = jnp.exp(-t * dl_ref[...])
    win = jnp.where(n == seq, 0.0, win)
    wdt = o_ref.shape[-1]
    o_ref[0] = filt[:, :wdt] * win
    o_ref[1] = filt[:, wdt:] * win


def hyena_filter(seq, w1p, b1, w2, b2, w3, b3, freq, w4d, deltas_abs):
    tr = 256
    wdt = deltas_abs.shape[-1]
    nblk = 2 * seq // tr
    half = seq // tr
    full = lambda shape: pl.BlockSpec(shape, lambda i: (0,) * len(shape))
    return pl.pallas_call(
        functools.partial(_filter_kernel, seq=seq, tr=tr),
        grid=(nblk,),
        in_specs=[full((LANE, HY_HID)), full((1, HY_HID)), full((HY_HID, HY_HID)), full((1, HY_HID)),
                  full((HY_HID, HY_HID)), full((1, HY_HID)), full((3, HY_HID)),
                  pl.BlockSpec((None, HY_HID, 2 * wdt), lambda i: (jnp.where(i < half, 0, 1), 0, 0)),
                  full((1, wdt))],
        out_specs=pl.BlockSpec((2, tr, wdt), lambda i: (0, i, 0)),
        out_shape=jax.ShapeDtypeStruct((2, 2 * seq, wdt), F32),
        compiler_params=_cp("arbitrary"), name="hyena_filter",
    )(w1p, b1, w2, b2, w3, b3, freq, w4d, deltas_abs)


def _dft_num_c(n1):
    return -(-(n1 // 2 + 1) // SUB) * SUB


def _dft_tables(n1):
    n = n1 * DFT_MINOR
    nc = _dft_num_c(n1)
    two_pi = 2.0 * math.pi
    ia = jnp.arange(n1, dtype=jnp.int32)
    ic = jnp.arange(nc, dtype=jnp.int32)
    ib = jnp.arange(DFT_MINOR, dtype=jnp.int32)
    m = (ic[None, :, None] * (DFT_MINOR * ia[None, None, :] + ib[:, None, None])) % n
    ang = m.astype(F32) * (two_pi / n)
    g = jnp.concatenate([jnp.cos(ang), -jnp.sin(ang)], axis=1).astype(BF16)
    th = ((ib[:, None] * ib[None, :]) % DFT_MINOR).astype(F32) * (two_pi / DFT_MINOR)
    c, s = jnp.cos(th), jnp.sin(th)
    mf = jnp.concatenate([jnp.concatenate([c, s], 1), jnp.concatenate([-s, c], 1)], 0).astype(BF16)
    kk = ic[:, None, None] + n1 * ib[None, None, :]
    mi = (ib[None, :, None] * kk) % n
    ps = mi.astype(F32) * (two_pi / n)
    cr, ci = jnp.cos(ps), jnp.sin(ps)
    cinv = jnp.concatenate([jnp.concatenate([cr, -ci], 2), jnp.concatenate([ci, cr], 2)], 1).astype(BF16)
    ph = ((ia[: n1 // 2, None] * ic[None, :]) % n1).astype(F32) * (two_pi / n1)
    wc = jnp.where((ic == 0) | (ic == n1 // 2), 1.0, jnp.where(ic < n1 // 2, 2.0, 0.0)) * (1.0 / n)
    m3 = (jnp.concatenate([jnp.cos(ph), -jnp.sin(ph)], 1) * jnp.tile(wc, 2)[None, :]).astype(BF16)
    return g, mf, cinv, m3


def _s1_kernel(g_ref, x_ref, o_ref):
    for j in range(SUB):
        o_ref[:, j, :] = _dot(g_ref[j], x_ref[:, j, :].astype(BF16))


def dft_stage1(x4, g, comp, rows, wdt, name):
    gsz = x4.shape[0]
    n1x2 = g.shape[1]
    return pl.pallas_call(
        _s1_kernel, grid=(gsz, DFT_MINOR // SUB),
        in_specs=[pl.BlockSpec((SUB, n1x2, rows), lambda i, bb: (bb, 0, 0)),
                  pl.BlockSpec((None, rows, SUB, wdt), lambda i, bb: (i, 0, bb, comp))],
        out_specs=pl.BlockSpec((None, n1x2, SUB, wdt), lambda i, bb: (i, 0, bb, 0)),
        out_shape=jax.ShapeDtypeStruct((gsz, n1x2, DFT_MINOR, wdt), F32),
        compiler_params=_cp("arbitrary", "arbitrary"), name=name,
    )(g, x4)


def _s2f_kernel(mf_ref, a_ref, o_ref):
    wdt = a_ref.shape[-1]
    for j in range(SUB):
        y = _dot(mf_ref[...], a_ref[:, j].astype(BF16).reshape(2 * DFT_MINOR, wdt))
        o_ref[:, j] = y.reshape(2, DFT_MINOR, wdt)


def filter_spectrum(a5, mf):
    no, _, n1, _, wdt = a5.shape
    ct = 512
    spec = pl.BlockSpec((None, 2, SUB, DFT_MINOR, ct), lambda o, c, jc: (o, 0, c, 0, jc))
    return pl.pallas_call(
        _s2f_kernel, grid=(no, n1 // SUB, wdt // ct),
        in_specs=[pl.BlockSpec((2 * DFT_MINOR, 2 * DFT_MINOR), lambda o, c, jc: (0, 0)), spec],
        out_specs=spec,
        out_shape=jax.ShapeDtypeStruct(a5.shape, F32),
        compiler_params=_cp("arbitrary", "arbitrary", "arbitrary"), name="hyena_filter_spectrum",
    )(mf, a5)


def _s2_kernel(mf_ref, ci_ref, h_ref, a_ref, o_ref):
    wdt = a_ref.shape[-1]
    for j in range(SUB):
        y = _dot(mf_ref[...], a_ref[:, j].astype(BF16).reshape(2 * DFT_MINOR, wdt))
        yr, yi = y[:DFT_MINOR], y[DFT_MINOR:]
        hr, hi = h_ref[0, j], h_ref[1, j]
        z = jnp.concatenate([yr * hr - yi * hi, yr * hi + yi * hr], axis=0).astype(BF16)
        o_ref[:, j, :] = _dot(ci_ref[j], z)


def spectral_multiply(a5, spec5, order, mf, cinv):
    bsz, _, n1, _, wdt = a5.shape
    ct = 512
    return pl.pallas_call(
        _s2_kernel, grid=(n1 // SUB, bsz, wdt // ct),
        in_specs=[pl.BlockSpec((2 * DFT_MINOR, 2 * DFT_MINOR), lambda c, bi, jc: (0, 0)),
                  pl.BlockSpec((SUB, 2 * DFT_MINOR, 2 * DFT_MINOR), lambda c, bi, jc: (c, 0, 0)),
                  pl.BlockSpec((None, 2, SUB, DFT_MINOR, ct), lambda c, bi, jc: (order, 0, c, 0, jc)),
                  pl.BlockSpec((None, 2, SUB, DFT_MINOR, ct), lambda c, bi, jc: (bi, 0, c, 0, jc))],
        out_specs=pl.BlockSpec((None, 2 * DFT_MINOR, SUB, ct), lambda c, bi, jc: (bi, 0, c, jc)),
        out_shape=jax.ShapeDtypeStruct((bsz, 2 * DFT_MINOR, n1, wdt), F32),
        compiler_params=_cp("arbitrary", "arbitrary", "arbitrary"), name="hyena_spectral_multiply",
    )(mf, cinv, spec5, a5)


def _s3_mid_kernel(m3_ref, g_ref, bq_ref, z_ref, x_ref, sk_ref, zo_ref, ao_ref):
    wdt = bq_ref.shape[-1]
    for j in range(SUB):
        y = _dot(m3_ref[...], bq_ref[:, j].astype(BF16).reshape(-1, wdt))
        z = x_ref[:, j, :] * (y + z_ref[:, j, :] * sk_ref[...])
        zo_ref[:, j, :] = z
        ao_ref[:, j, :] = _dot(g_ref[j], z.astype(BF16))


def _s3_last_kernel(m3_ref, bq_ref, z_ref, x_ref, sk_ref, gate_ref, o_ref):
    wdt = bq_ref.shape[-1]
    for j in range(SUB):
        y = _dot(m3_ref[...], bq_ref[:, j].astype(BF16).reshape(-1, wdt))
        z = x_ref[:, j, :] * (y + z_ref[:, j, :] * sk_ref[...])
        o_ref[:, j, :] = z * _silu(gate_ref[:, j, :])


def _hyena_ctx_kernel(ff_ref, fd_ref, fi_ref, circ_ref, v_ref, x1_ref, x2_ref, gate_ref, sk_ref, o_ref, *, nc):
    z = v_ref[...].astype(F32)
    xs = (x1_ref, x2_ref)
    for o in range(2):
        hs = _dot(ff_ref[...], circ_ref[o].astype(BF16))
        us = _dot(fd_ref[...], z.astype(BF16))
        hr, hi = hs[:nc], hs[nc:]
        ur, ui = us[:nc], us[nc:]
        zz = jnp.concatenate([ur * hr - ui * hi, ur * hi + ui * hr], axis=0).astype(BF16)
        y = _dot(fi_ref[...], zz)
        z = xs[o][...].astype(F32) * (y + z * sk_ref[o:o + 1, :])
    o_ref[...] = (z * _silu(gate_ref[...].astype(F32))).astype(o_ref.dtype)


def hyena_context(hv, p, circ_c, skip, n_lat_rows, lc):
    bsz, _, w3 = hv.shape
    wdt = w3 // 3
    nc = 2 * lc
    ct = 256
    two_pi = 2.0 * math.pi
    ik = jnp.arange(nc, dtype=jnp.int32)
    th = ((ik[:, None] * ik[None, :]) % nc).astype(F32) * (two_pi / nc)
    c, s = jnp.cos(th), jnp.sin(th)
    ffull = jnp.concatenate([c, -s], axis=0).astype(BF16)
    fdata = ffull[:, :lc]
    finv = (jnp.concatenate([c[:lc], -s[:lc]], axis=1) * (1.0 / nc)).astype(BF16)
    rb = n_lat_rows // lc
    cw = wdt // ct
    full = lambda shape: pl.BlockSpec(shape, lambda bi, j: (0,) * len(shape))
    return pl.pallas_call(
        functools.partial(_hyena_ctx_kernel, nc=nc),
        grid=(bsz, cw),
        in_specs=[full((2 * nc, nc)), full((2 * nc, lc)), full((lc, 2 * nc)),
                  pl.BlockSpec((2, nc, ct), lambda bi, j: (0, 0, j)),
                  pl.BlockSpec((None, lc, ct), lambda bi, j: (bi, rb, j)),
                  pl.BlockSpec((None, lc, ct), lambda bi, j: (bi, rb, cw + j)),
                  pl.BlockSpec((None, lc, ct), lambda bi, j: (bi, rb, 2 * cw + j)),
                  pl.BlockSpec((None, lc, ct), lambda bi, j: (bi, rb, C_HYGATE // ct + j)),
                  pl.BlockSpec((2, ct), lambda bi, j: (0, j))],
        out_specs=pl.BlockSpec((None, lc, ct), lambda bi, j: (bi, 0, j)),
        out_shape=jax.ShapeDtypeStruct((bsz, lc, wdt), BF16),
        compiler_params=_cp("arbitrary", "arbitrary"), name="hyena_context",
    )(ffull, fdata, finv, circ_c, hv, hv, hv, p, skip)


def hyena_latent(hv, p, circ_l, skip, n_lat_rows, tables):
    g, mf, cinv, m3 = tables
    bsz, t, w3 = hv.shape
    wdt = w3 // 3
    n1 = 2 * n_lat_rows // DFT_MINOR
    nc = _dft_num_c(n1)
    half = n1 // 2
    ct = 512
    nj = wdt // ct
    af = dft_stage1(circ_l.reshape(2, n1, DFT_MINOR, wdt), g, 0, n1, wdt, "hyena_filter_stage1")
    spec5 = filter_spectrum(af.reshape(2, 2, nc, DFT_MINOR, wdt), mf)
    gd = g[:, :, :half]
    hv4 = hv.reshape(bsz, t // DFT_MINOR, DFT_MINOR, w3)
    p4 = p.reshape(bsz, t // DFT_MINOR, DFT_MINOR, p.shape[-1])
    sk = skip.reshape(2, 1, wdt)
    a = dft_stage1(hv4, gd, 0, half, wdt, "hyena_stage1")
    bq = spectral_multiply(a.reshape(bsz, 2, nc, DFT_MINOR, wdt), spec5, 0, mf, cinv)
    grid = (bsz, DFT_MINOR // SUB, nj)
    nat = lambda cb: pl.BlockSpec((None, half, SUB, ct), lambda bi, bb, jc: (bi, 0, bb, cb + jc))
    bq_spec = pl.BlockSpec((None, 2, SUB, nc, ct), lambda bi, bb, jc: (bi, 0, bb, 0, jc))
    m3_spec = pl.BlockSpec((half, 2 * nc), lambda bi, bb, jc: (0, 0))
    sk_spec = lambda o: pl.BlockSpec((None, 1, ct), lambda bi, bb, jc: (o, 0, jc))
    z1, a = pl.pallas_call(
        _s3_mid_kernel, grid=grid,
        in_specs=[m3_spec, pl.BlockSpec((SUB, 2 * nc, half), lambda bi, bb, jc: (bb, 0, 0)),
                  bq_spec, nat(0), nat(nj), sk_spec(0)],
        out_specs=[nat(0), pl.BlockSpec((None, 2 * nc, SUB, ct), lambda bi, bb, jc: (bi, 0, bb, jc))],
        out_shape=[jax.ShapeDtypeStruct((bsz, half, DFT_MINOR, wdt), F32),
                   jax.ShapeDtypeStruct((bsz, 2 * nc, DFT_MINOR, wdt), F32)],
        compiler_params=_cp("arbitrary", "arbitrary", "arbitrary"), name="hyena_stage3_mid",
    )(m3, gd, bq.reshape(bsz, 2, DFT_MINOR, nc, wdt), hv4, hv4, sk)
    bq = spectral_multiply(a.reshape(bsz, 2, nc, DFT_MINOR, wdt), spec5, 1, mf, cinv)
    y = pl.pallas_call(
        _s3_last_kernel, grid=grid,
        in_specs=[m3_spec, bq_spec, nat(0), nat(2 * nj), sk_spec(1), nat(C_HYGATE // ct)],
        out_specs=nat(0),
        out_shape=jax.ShapeDtypeStruct((bsz, half, DFT_MINOR, wdt), F32),
        compiler_params=_cp("arbitrary", "arbitrary", "arbitrary"), name="hyena_stage3_last",
    )(m3, bq.reshape(bsz, 2, DFT_MINOR, nc, wdt), z1, hv4, sk, p4)
    return y.reshape(bsz, n_lat_rows, wdt)


SCAN_ROWS = 256


def _scan_block_map(direction, n_lat_rows, t_rows):
    n_lat, n_all = n_lat_rows // SCAN_ROWS, t_rows // SCAN_ROWS
    if direction == 0:
        return lambda t: (t + n_lat) % n_all
    return lambda t: n_all - 1 - t


def _sub_chunks(rows, chunk, direction):
    order = range(rows // chunk) if direction == 0 else reversed(range(rows // chunk))
    return [slice(i * chunk, (i + 1) * chunk) for i in order]


def _tri_mask(n, direction):
    ri = lax.broadcasted_iota(jnp.int32, (n, n), 0)
    ci = lax.broadcasted_iota(jnp.int32, (n, n), 1)
    return (ci <= ri) if direction == 0 else (ci >= ri)


def _expand_heads(v, lane0):
    rows = v.shape[0]
    low_half = lax.broadcasted_iota(jnp.int32, (1, LANE), 1) < SSD_HEADDIM
    tiles = []
    for k in range(SSD_HEADS // 2):
        lo = jnp.broadcast_to(v[:, lane0 + 2 * k:lane0 + 2 * k + 1], (rows, LANE))
        hi = jnp.broadcast_to(v[:, lane0 + 2 * k + 1:lane0 + 2 * k + 2], (rows, LANE))
        tiles.append(jnp.where(low_half, lo, hi))
    return jnp.concatenate(tiles, axis=1)


def _ssd_kernel(*refs, direction):
    d = direction
    if d == 0:
        xs_ref, bm_ref, cm_ref, sm_ref, dtb_ref, alog_ref, o_ref, st_ref = refs
    else:
        (xs_ref, bm_ref, cm_ref, sm_ref, dtb_ref, alog_ref,
         y0_ref, z_ref, dsk_ref, nw_ref, o_ref, st_ref) = refs
    q = SSD_CHUNK
    hd = SSD_HEADDIM
    gw = SSD_RPG * hd

    @pl.when(pl.program_id(1) == 0)
    def _():
        st_ref[...] = jnp.zeros_like(st_ref)

    tri = _tri_mask(q, d)
    trif = tri.astype(F32)
    neg_a = -jnp.exp(alog_ref[...])
    for rs in _sub_chunks(xs_ref.shape[0], q, d):
        xs = xs_ref[rs, :].astype(F32)
        bm = bm_ref[rs, :].astype(BF16)
        cm = cm_ref[rs, :].astype(BF16)
        dt = _softplus(sm_ref[rs, :] + dtb_ref[...])
        a = dt * neg_a
        acum = _dot_hi(trif, a)
        acum_t = acum.T
        dtx = _expand_heads(dt, SSD_HEADS * d)
        ax = _expand_heads(acum, SSD_HEADS * d)
        atx = ax[q - 1:q, :] if d == 0 else ax[0:1, :]
        xdt = xs * dtx
        e_in = jnp.exp(ax)
        xw = (xdt * jnp.exp(atx - ax)).astype(BF16)
        xdt_b = xdt.astype(BF16)
        ys = []
        for g in range(SSD_GROUPS):
            bg = bm[:, g * SSD_STATE:(g + 1) * SSD_STATE]
            cg = cm[:, g * SSD_STATE:(g + 1) * SSD_STATE]
            sc = _nt(cg, bg)
            st_g = st_ref[g * gw:(g + 1) * gw, :]
            y_int = _nt(cg, st_g.astype(BF16))
            yh, decs = [], []
            for r in range(SSD_RPG):
                hl = SSD_HEADS * d + SSD_RPG * g + r
                seg = acum[:, hl:hl + 1] - acum_t[hl:hl + 1, :]
                dm = jnp.exp(jnp.where(tri, seg, -1e30))
                ch = (SSD_RPG * g + r) * hd
                yh.append(_dot((sc * dm).astype(BF16), xdt_b[:, ch:ch + hd]))
                a_tot = acum[q - 1:q, hl:hl + 1] if d == 0 else acum[0:1, hl:hl + 1]
                decs.append(jnp.broadcast_to(jnp.exp(a_tot), (hd, SSD_STATE)))
            ys.append(jnp.concatenate(yh, axis=1) + y_int * e_in[:, g * gw:(g + 1) * gw])
            st_ref[g * gw:(g + 1) * gw, :] = (st_g * jnp.concatenate(decs, axis=0)
                                              + _tn(xw[:, g * gw:(g + 1) * gw], bg))
        y = jnp.concatenate(ys, axis=1)
        if d == 0:
            o_ref[rs, :] = y
        else:
            y = (y0_ref[rs, :] + y + dsk_ref[...] * xs) * _silu(z_ref[rs, :].astype(F32))
            parts = []
            for g in range(SSD_GROUPS):
                yg = y[:, g * gw:(g + 1) * gw]
                parts.append(yg * lax.rsqrt(jnp.mean(yg * yg, axis=-1, keepdims=True) + EPS))
            o_ref[rs, :] = (jnp.concatenate(parts, axis=1) * nw_ref[...]).astype(o_ref.dtype)


def ssd_mixer(xbc, p, psmall, dtb, alog, dskip, normw, n_lat_rows):
    bsz, t, _ = xbc.shape
    q = SCAN_ROWS
    w = BRANCH_W
    nct = t // q
    outs = None
    for d in range(2):
        cm_ = _scan_block_map(d, n_lat_rows, t)
        row = lambda width, cb, cm_=cm_: pl.BlockSpec((None, q, width), lambda bi, ti: (bi, cm_(ti), cb))
        const = lambda shape: pl.BlockSpec(shape, lambda bi, ti: (0,) * len(shape))
        in_specs = [row(w, 0), row(4 * SSD_STATE, 2), row(4 * SSD_STATE, 3), row(NSMALL, 0),
                    const((1, NSMALL)), const((1, NSMALL))]
        args = [xbc, xbc, xbc, psmall, dtb, alog]
        if d == 1:
            in_specs += [row(w, 0), row(w, C_SSDZ // w), const((1, w)), const((1, w))]
            args += [outs, p, dskip, normw]
        outs = pl.pallas_call(
            functools.partial(_ssd_kernel, direction=d), grid=(bsz, nct),
            in_specs=in_specs, out_specs=row(w, 0),
            out_shape=jax.ShapeDtypeStruct((bsz, t, w), F32 if d == 0 else BF16),
            scratch_shapes=[pltpu.VMEM((SSD_HEADS * SSD_HEADDIM, SSD_STATE), F32)],
            compiler_params=_cp("arbitrary", "arbitrary"), name=f"ssd_scan_dir{d}",
        )(*args)
    return outs


def _gla_kernel(*refs, direction):
    d = direction
    if d == 0:
        q_ref, k_ref, v_ref, sm_ref, w2_ref, b2_ref, o_ref, st_ref = refs
    else:
        q_ref, k_ref, v_ref, sm_ref, w2_ref, b2_ref, y0_ref, g_ref, nw_ref, o_ref, st_ref = refs
    cs = GLA_CHUNK
    dk, dv = GLA_HDK, GLA_HDV

    @pl.when(pl.program_id(1) == 0)
    def _():
        st_ref[...] = jnp.zeros_like(st_ref)

    tri = _tri_mask(cs, d)
    trif = tri.astype(F32)
    mid = cs // 2 if d == 0 else cs - 1 - cs // 2
    for rs in _sub_chunks(q_ref.shape[0], cs, d):
        logit = _dot(sm_ref[rs, :].astype(BF16), w2_ref[...]) + b2_ref[...]
        gl = _log_sigmoid(logit) * (1.0 / GLA_NORMALIZER)
        gc = _dot_hi(trif, gl)
        g_mid = gc[mid:mid + 1, :]
        g_last = gc[cs - 1:cs, :] if d == 0 else gc[0:1, :]
        qf = q_ref[rs, :].astype(F32) * (dk ** -0.5)
        kf = k_ref[rs, :].astype(F32)
        vb = v_ref[rs, :].astype(BF16)
        qa = (qf * jnp.exp(gc - g_mid)).astype(BF16)
        ka = (kf * jnp.exp(g_mid - gc)).astype(BF16)
        qs = (qf * jnp.exp(gc)).astype(BF16)
        ke = (kf * jnp.exp(g_last - gc)).astype(BF16)
        dec = jnp.exp(g_last)
        ys = []
        for h in range(GLA_HEADS):
            ks, vs = slice(h * dk, (h + 1) * dk), slice(h * dv, (h + 1) * dv)
            att = jnp.where(tri, _nt(qa[:, ks], ka[:, ks]), 0.0)
            st_h = st_ref[h]
            ys.append(_dot(att.astype(BF16), vb[:, vs]) + _nt(qs[:, ks], st_h.astype(BF16)))
            st_ref[h] = st_h * dec[:, ks] + _tn(vb[:, vs], ke[:, ks])
        y = jnp.concatenate(ys, axis=1)
        if d == 0:
            o_ref[rs, :] = y
        else:
            y = y0_ref[rs, :] + y
            parts = []
            for h in range(GLA_HEADS):
                yh = y[:, h * dv:(h + 1) * dv]
                parts.append(yh * lax.rsqrt(jnp.mean(yh * yh, axis=-1, keepdims=True) + EPS))
            y = jnp.concatenate(parts, axis=1) * nw_ref[...]
            o_ref[rs, :] = (y * _silu(g_ref[rs, :].astype(F32))).astype(o_ref.dtype)


def gla_mixer(p, psmall, w2p, b2, normw, n_lat_rows):
    bsz, t, _ = p.shape
    cs = SCAN_ROWS
    w = BRANCH_W
    hw = GLA_HEADS * GLA_HDK
    nct = t // cs
    outs = None
    for d in range(2):
        cm_ = _scan_block_map(d, n_lat_rows, t)
        row = lambda width, cb, cm_=cm_: pl.BlockSpec((None, cs, width), lambda bi, ti: (bi, cm_(ti), cb))
        const = lambda shape: pl.BlockSpec(shape, lambda bi, ti: (0,) * len(shape))
        in_specs = [row(hw, C_GQ // hw), row(hw, C_GK // hw), row(w, C_GV // w), row(NSMALL, 0),
                    pl.BlockSpec((None, NSMALL, hw), lambda bi, ti, d=d: (d, 0, 0)),
                    pl.BlockSpec((None, 1, hw), lambda bi, ti, d=d: (d, 0, 0))]
        args = [p, p, p, psmall, w2p, b2]
        if d == 1:
            in_specs += [row(w, 0), row(w, C_GG // w), const((1, w))]
            args += [outs, p, normw]
        outs = pl.pallas_call(
            functools.partial(_gla_kernel, direction=d), grid=(bsz, nct),
            in_specs=in_specs, out_specs=row(w, 0),
            out_shape=jax.ShapeDtypeStruct((bsz, t, w), F32 if d == 0 else BF16),
            scratch_shapes=[pltpu.VMEM((GLA_HEADS, GLA_HDV, GLA_HDK), F32)],
            compiler_params=_cp("arbitrary", "arbitrary"), name=f"gla_scan_dir{d}",
        )(*args)
    return outs


def _rope(x, cos, sin_signed):
    half = RET_HD // 2
    lo, hi = x[:, :half], x[:, half:]
    lo = lo * cos[:, :half] + pltpu.roll(lo, half // 2, 1) * sin_signed[:, :half]
    hi = hi * cos[:, half:] + pltpu.roll(hi, half // 2, 1) * sin_signed[:, half:]
    return jnp.concatenate([lo, hi], axis=1)


def _ret_kernel(*refs, direction):
    d = direction
    if d == 0:
        q_ref, k_ref, v_ref, cos_ref, sin_ref, dr_ref, o_ref, st_ref = refs
    else:
        q_ref, k_ref, v_ref, cos_ref, sin_ref, dr_ref, y0_ref, g_ref, o_ref, st_ref = refs
    cs = RET_CHUNK
    hdim = RET_HD

    @pl.when(pl.program_id(1) == 0)
    def _():
        st_ref[...] = jnp.zeros_like(st_ref)

    lam_all = -jnp.exp(dr_ref[...])
    tri = _tri_mask(cs, d)
    ri = lax.broadcasted_iota(jnp.int32, (cs, cs), 0)
    ci = lax.broadcasted_iota(jnp.int32, (cs, cs), 1)
    lag = jnp.abs(ri - ci).astype(F32)
    pos = lax.broadcasted_iota(jnp.int32, (cs, hdim), 0).astype(F32)
    steps_in = (pos + 1.0) if d == 0 else (cs - pos)
    steps_out = (cs - 1.0 - pos) if d == 0 else pos
    lams = [lam_all[d:d + 1, h:h + 1] for h in range(RET_HEADS)]
    dms = [jnp.where(tri, jnp.exp(lam * lag), 0.0) for lam in lams]
    w_in = [jnp.exp(lam * steps_in) for lam in lams]
    w_out = [jnp.exp(lam * steps_out) for lam in lams]
    w_chunk = [jnp.exp(lam * cs) for lam in lams]
    for rs in _sub_chunks(q_ref.shape[0], cs, d):
        cos, sin_s = cos_ref[rs, :], sin_ref[rs, :]
        qf = q_ref[rs, :].astype(F32)
        kf = k_ref[rs, :].astype(F32) * (hdim ** -0.5)
        vb = v_ref[rs, :].astype(BF16)
        ys = []
        for h in range(RET_HEADS):
            hs = slice(h * hdim, (h + 1) * hdim)
            qh = _rope(qf[:, hs], cos, sin_s)
            kh = _rope(kf[:, hs], cos, sin_s)
            qb = qh.astype(BF16)
            att = (_nt(qb, kh.astype(BF16)) * dms[h]).astype(BF16)
            st_h = st_ref[h]
            ys.append(_dot(att, vb[:, hs]) + _dot(qb, st_h.astype(BF16)) * w_in[h])
            st_ref[h] = st_h * w_chunk[h] + _tn((kh * w_out[h]).astype(BF16), vb[:, hs])
        y = jnp.concatenate(ys, axis=1)
        if d == 0:
            o_ref[rs, :] = y
        else:
            y = y0_ref[rs, :] + y
            y = jnp.concatenate([_ln_rows(y[:, h * hdim:(h + 1) * hdim]) for h in range(RET_HEADS)], axis=1)
            o_ref[rs, :] = (y * _silu(g_ref[rs, :].astype(F32))).astype(o_ref.dtype)


def ret_mixer(p, cos_t, sin_t, decay_pad, n_lat_rows):
    bsz, t, _ = p.shape
    cs = SCAN_ROWS
    w = BRANCH_W
    nct = t // cs
    outs = None
    for d in range(2):
        cm_ = _scan_block_map(d, n_lat_rows, t)
        row = lambda cb, cm_=cm_: pl.BlockSpec((None, cs, w), lambda bi, ti: (bi, cm_(ti), cb))
        tab = pl.BlockSpec((cs, RET_HD), lambda bi, ti, cm_=cm_: (cm_(ti), 0))
        in_specs = [row(C_RQ // w), row(C_RK // w), row(C_RV // w), tab, tab,
                    pl.BlockSpec((8, LANE), lambda bi, ti: (0, 0))]
        args = [p, p, p, cos_t, sin_t, decay_pad]
        if d == 1:
            in_specs += [row(0), row(C_RG // w)]
            args += [outs, p]
        outs = pl.pallas_call(
            functools.partial(_ret_kernel, direction=d), grid=(bsz, nct),
            in_specs=in_specs, out_specs=row(0),
            out_shape=jax.ShapeDtypeStruct((bsz, t, w), F32 if d == 0 else BF16),
            scratch_shapes=[pltpu.VMEM((RET_HEADS, RET_HD, RET_HD), F32)],
            compiler_params=_cp("arbitrary", "arbitrary"), name=f"ret_scan_dir{d}",
        )(*args)
    return outs


def _rope_tables(n_lat_rows, lc):
    half = RET_HD // 2
    inv = ROPE_BASE ** (-jnp.arange(0, half, 2, dtype=F32) / half)
    tpos = jnp.arange(n_lat_rows)
    row = (tpos // GRID_W).astype(F32)[:, None] * inv[None, :]
    col = (tpos % GRID_W).astype(F32)[:, None] * inv[None, :]
    cos = jnp.concatenate([jnp.cos(row)] * 2 + [jnp.cos(col)] * 2, axis=1)
    sin = jnp.concatenate([-jnp.sin(row), jnp.sin(row), -jnp.sin(col), jnp.sin(col)], axis=1)
    cos = jnp.concatenate([cos, jnp.ones((lc, RET_HD), F32)], axis=0)
    sin = jnp.concatenate([sin, jnp.zeros((lc, RET_HD), F32)], axis=0)
    return cos, sin


def _pad_lanes(v, start):
    v = v.reshape(-1).astype(F32)
    return jnp.zeros((1, NSMALL), F32).at[0, start:start + v.shape[0]].set(v)


def kernel(x, c, ctx, c_ctx, w_ada, b_ada, w_in, hy_conv_w, hy_conv_b, hy_w1, hy_b1, hy_w2, hy_b2, hy_w3,
           hy_b3, hy_w4, hy_freq, hy_skip, ssd_conv_w, ssd_conv_b, ssd_a_log, ssd_dt_bias, ssd_d, ssd_norm_w,
           gla_w2, gla_b2, gla_norm_w, ret_decay, w_gate, w_br, w_out, ln_g, ln_b):
    bsz, n_lat, d = x.shape
    lc = ctx.shape[1]
    t = n_lat + lc
    depth = w_in.shape[0]
    w = BRANCH_W
    assert bsz <= 2 and d == D_MODEL and n_lat % 256 == 0 and lc % 256 == 0

    s = jnp.concatenate([x, ctx], axis=1)
    cs = jnp.zeros((8, d), F32).at[:bsz].set(c).at[2].set(c_ctx)
    mod = ada_modulation(cs, w_ada, b_ada).reshape(depth, 8, 1, 3 * d)

    cos_t, sin_t = _rope_tables(n_lat, lc)
    tables = _dft_tables(2 * n_lat // DFT_MINOR)
    deltas = jnp.abs(jnp.linspace(math.log(HY_TARGET) / HY_SLOW, math.log(HY_TARGET) / HY_FAST, w,
                                  dtype=F32)).reshape(1, w)

    w_gate_b, w_br_b, w_out_b = w_gate.astype(BF16), w_br.astype(BF16), w_out.astype(BF16)
    h = ln_modulate(s, mod[0], None, None, n_lat, pre_ln=False, emit_h=True)[0]
    for l in range(depth):
        wi = w_in[l]
        w_main = jnp.concatenate([wi[:, 0:6144], wi[:, 6176:9248], wi[:, 9280:14400]], axis=1).astype(BF16)
        w_small = jnp.concatenate([wi[:, 6144:6176], wi[:, 9248:9280], jnp.zeros((d, NSMALL - 64), F32)],
                                  axis=1).astype(BF16)
        h2 = h.reshape(bsz * t, d)
        p = matmul(h2, w_main, F32, "in_projection").reshape(bsz, t, NP)
        psmall = matmul(h2, w_small, F32, "in_projection_small").reshape(bsz, t, NSMALL)

        hv = short_conv(p, C_HYIN, 3 * w, hy_conv_w[l], hy_conv_b[l], n_lat, False, F32, "hyena_short_conv")
        w1p = jnp.zeros((LANE, HY_HID), F32).at[:2 * HY_BANDS + 1].set(hy_w1[l])
        w4d = hy_w4[l].reshape(HY_HID, 2, 2, w).transpose(2, 0, 1, 3).reshape(2, HY_HID, 2 * w)
        fargs = (w1p, hy_b1[l].reshape(1, -1), hy_w2[l], hy_b2[l].reshape(1, -1), hy_w3[l],
                 hy_b3[l].reshape(1, -1), hy_freq[l], w4d, deltas)
        circ_l = hyena_filter(n_lat, *fargs)
        circ_c = hyena_filter(lc, *fargs)
        y_hy = jnp.concatenate([hyena_latent(hv, p, circ_l, hy_skip[l], n_lat, tables).astype(BF16),
                                hyena_context(hv, p, circ_c, hy_skip[l], n_lat, lc)], axis=1)

        xbc = short_conv(p, C_XBC, 2 * w, ssd_conv_w[l], ssd_conv_b[l], n_lat, True, F32, "ssd_short_conv")
        y_ssd = ssd_mixer(xbc, p, psmall, _pad_lanes(ssd_dt_bias[l], 0), _pad_lanes(ssd_a_log[l], 0),
                          jnp.repeat(ssd_d[l].astype(F32), SSD_HEADDIM).reshape(1, w),
                          ssd_norm_w[l].astype(F32).reshape(1, w), n_lat)

        hw = GLA_HEADS * GLA_HDK
        w2p = jnp.zeros((2, NSMALL, hw), F32)
        for dd in range(2):
            w2p = w2p.at[dd, 32 + GLA_RANK * dd:32 + GLA_RANK * (dd + 1)].set(gla_w2[l, dd])
        y_gla = gla_mixer(p, psmall, w2p.astype(BF16), gla_b2[l].astype(F32).reshape(2, 1, hw),
                          jnp.tile(gla_norm_w[l].astype(F32), GLA_HEADS).reshape(1, w), n_lat)

        decay_pad = jnp.zeros((8, LANE), F32).at[:2, :RET_HEADS].set(ret_decay[l])
        y_ret = ret_mixer(p, cos_t, sin_t, decay_pad, n_lat)

        ycat = jnp.concatenate([y_hy, y_ssd, y_gla, y_ret], axis=-1).reshape(bsz * t, 4 * w)
        m = gated_merge(h2, ycat, w_gate_b, w_br_b, l)
        gate_rows = mod[l][:, :, 2 * d:3 * d]
        pre = out_projection(m, w_out_b, l, s.reshape(bsz * t, d), gate_rows, t, n_lat)
        pre = pre.reshape(bsz, t, d)
        if l + 1 < depth:
            s, h = ln_modulate(pre, mod[l + 1], ln_g[l], ln_b[l], n_lat, pre_ln=True, emit_h=True)
        else:
            s = ln_modulate(pre, None, ln_g[l], ln_b[l], n_lat, pre_ln=True, emit_h=False, rows=n_lat)[0]
    return s
```

```python
import functools
import math

import jax
import jax.numpy as jnp
from jax import lax
from jax.experimental import pallas as pl
from jax.experimental.pallas import tpu as pltpu

F32 = jnp.float32
BF16 = jnp.bfloat16
HI = lax.Precision.HIGHEST

D_MODEL = 4096
DEPTH = 2
BRANCH_W = 1024
GRID_W = 64
EPS = 1e-6
ALPHA = (2 * DEPTH) ** 0.25

HY_BANDS = 16
HY_HID = 64
HY_TARGET = 1e-2
HY_FAST = 0.3
HY_SLOW = 1.5

SSD_HEADS = 16
SSD_HEADDIM = 64
SSD_GROUPS = 4
SSD_RPG = 4
SSD_STATE = 128
SSD_CHUNK = 128

GLA_HEADS = 4
GLA_HDK = 128
GLA_HDV = 256
GLA_RANK = 16
GLA_NORMALIZER = 16.0
GLA_CHUNK = 64

RET_HEADS = 4
RET_HD = 256
RET_CHUNK = 128
ROPE_BASE = 10000.0

NP = 14336
C_HYIN, C_HYGATE, C_XBC, C_SSDZ = 0, 3072, 4096, 6144
C_GQ, C_GK, C_GV, C_GG = 7168, 7680, 8192, 9216
C_RQ, C_RK, C_RV, C_RG = 10240, 11264, 12288, 13312
NSMALL = 128
LANE = 128
DFT_MINOR = 128
SUB = 8

VMEM_LIMIT = 52 * 1024 * 1024


def _cp(*sem):
    return pltpu.CompilerParams(dimension_semantics=sem, vmem_limit_bytes=VMEM_LIMIT)


def _pick(n, cands):
    for c in cands:
        if n % c == 0:
            return c
    raise ValueError(f"no tile for {n} in {cands}")


def _silu(x):
    return x * jax.nn.sigmoid(x)


def _softplus(x):
    return jnp.maximum(x, 0.0) + jnp.log1p(jnp.exp(-jnp.abs(x)))


def _log_sigmoid(x):
    return jnp.minimum(x, 0.0) - jnp.log1p(jnp.exp(-jnp.abs(x)))


def _nt(a, b):
    return lax.dot_general(a, b, (((1,), (1,)), ((), ())), preferred_element_type=F32)


def _tn(a, b):
    return lax.dot_general(a, b, (((0,), (0,)), ((), ())), preferred_element_type=F32)


def _dot(a, b):
    return jnp.dot(a, b, preferred_element_type=F32)


def _dot_hi(a, b):
    return jnp.dot(a, b, preferred_element_type=F32, precision=HI)


def _masked_cumsum(mask_b, x):
    hi = x.astype(BF16)
    r1 = x - hi.astype(F32)
    mid = r1.astype(BF16)
    lo = (r1 - mid.astype(F32)).astype(BF16)
    return _dot(mask_b, hi) + _dot(mask_b, mid) + _dot(mask_b, lo)


def _ada_kernel(c_ref, w_ref, b_ref, o_ref):
    a = _silu(c_ref[...]).astype(BF16)
    o_ref[...] = _dot(a, w_ref[...].astype(BF16)) + b_ref[...]


def ada_modulation(cs, w_ada, b_ada):
    depth, d, n = w_ada.shape
    tn = 512
    return pl.pallas_call(
        _ada_kernel,
        grid=(depth, n // tn),
        in_specs=[pl.BlockSpec((8, d), lambda l, j: (0, 0)),
                  pl.BlockSpec((None, d, tn), lambda l, j: (l, 0, j)),
                  pl.BlockSpec((None, 1, tn), lambda l, j: (l, 0, j))],
        out_specs=pl.BlockSpec((None, 8, tn), lambda l, j: (l, 0, j)),
        out_shape=jax.ShapeDtypeStruct((depth, 8, n), F32),
        compiler_params=_cp("arbitrary", "arbitrary"),
        name="ada_modulation",
    )(cs, w_ada, b_ada.reshape(depth, 1, n))


def _ln_rows(x):
    xc = x - jnp.mean(x, axis=-1, keepdims=True)
    return xc * lax.rsqrt(jnp.mean(xc * xc, axis=-1, keepdims=True) + EPS)


def _lnmod_kernel(*refs, pre_ln, emit_h, d, n_lat_blocks, split_src):
    it = iter(refs)
    s_ref = next(it)
    c_ref = next(it) if split_src else None
    mod_ref = next(it) if emit_h else None
    g_ref = next(it) if pre_ln else None
    b_ref = next(it) if pre_ln else None
    s_out = next(it) if (pre_ln or split_src) else None
    h_out = next(it) if emit_h else None
    x = s_ref[...]
    if split_src:
        x = jnp.where(pl.program_id(1) < n_lat_blocks, x, c_ref[...])
        s_out[...] = x
    if pre_ln:
        x = _ln_rows(x) * g_ref[...] + b_ref[...]
        s_out[...] = x
    if emit_h:
        shift = mod_ref[:, 0:d]
        scale = mod_ref[:, d:2 * d]
        h_out[...] = (_ln_rows(x) * (1.0 + scale) + shift).astype(BF16)


def ln_modulate(s, mod_rows, ln_g, ln_b, n_lat_rows, *, pre_ln, emit_h, rows=None, ctx=None):
    b, _, d = s.shape
    tr = 256
    nlat = n_lat_rows // tr
    split_src = ctx is not None
    assert not (split_src and pre_ln)
    if split_src:
        t = n_lat_rows + ctx.shape[1]
        args = [s, ctx]
        in_specs = [pl.BlockSpec((None, tr, d), lambda bi, ti: (bi, jnp.minimum(ti, nlat - 1), 0)),
                    pl.BlockSpec((None, tr, d), lambda bi, ti: (bi, jnp.maximum(ti - nlat, 0), 0))]
    else:
        t = s.shape[1] if rows is None else rows
        args = [s]
        in_specs = [pl.BlockSpec((None, tr, d), lambda bi, ti: (bi, ti, 0))]
    if emit_h:
        args.append(mod_rows)
        in_specs.append(pl.BlockSpec((None, 1, 3 * d), lambda bi, ti: (jnp.where(ti < nlat, bi, 2), 0, 0)))
    if pre_ln:
        args += [ln_g.reshape(1, d), ln_b.reshape(1, d)]
        in_specs += [pl.BlockSpec((1, d), lambda bi, ti: (0, 0))] * 2
    out_shape, out_specs = [], []
    if pre_ln or split_src:
        out_shape.append(jax.ShapeDtypeStruct((b, t, d), F32))
        out_specs.append(pl.BlockSpec((None, tr, d), lambda bi, ti: (bi, ti, 0)))
    if emit_h:
        out_shape.append(jax.ShapeDtypeStruct((b, t, d), BF16))
        out_specs.append(pl.BlockSpec((None, tr, d), lambda bi, ti: (bi, ti, 0)))
    return pl.pallas_call(
        functools.partial(_lnmod_kernel, pre_ln=pre_ln, emit_h=emit_h, d=d, n_lat_blocks=nlat,
                          split_src=split_src),
        grid=(b, t // tr), in_specs=in_specs, out_specs=out_specs, out_shape=out_shape,
        compiler_params=_cp("arbitrary", "arbitrary"), name="ln_modulate",
    )(*args)


def _mm_kernel(a_ref, w_ref, o_ref):
    o_ref[...] = _dot(a_ref[...], w_ref[...]).astype(o_ref.dtype)


def matmul(a, w, layer, out_dtype, name):
    m, k = a.shape
    n = w.shape[2]
    tm = _pick(m, (768, 512, 256))
    tn = _pick(n, (1024, 512, 128))
    return pl.pallas_call(
        _mm_kernel, grid=(m // tm, n // tn),
        in_specs=[pl.BlockSpec((tm, k), lambda i, j: (i, 0)),
                  pl.BlockSpec((None, k, tn), lambda i, j: (layer, 0, j))],
        out_specs=pl.BlockSpec((tm, tn), lambda i, j: (i, j)),
        out_shape=jax.ShapeDtypeStruct((m, n), out_dtype),
        compiler_params=_cp("arbitrary", "arbitrary"), name=name,
    )(a, w)


def _merge_kernel(h_ref, *refs):
    *y_refs, wg_ref, wb_ref, o_ref, acc_ref = refs
    i = pl.program_id(2)
    y = y_refs[-1][...]
    for idx in reversed(range(len(y_refs) - 1)):
        y = jnp.where(i == idx, y_refs[idx][...], y)
    term = jax.nn.sigmoid(_dot(h_ref[...], wg_ref[...])) * _dot(y, wb_ref[...])

    @pl.when(i == 0)
    def _():
        acc_ref[...] = term

    @pl.when(i > 0)
    def _():
        acc_ref[...] += term

    @pl.when(i == pl.num_programs(2) - 1)
    def _():
        o_ref[...] = acc_ref[...].astype(o_ref.dtype)


def gated_merge(h, ys, w_gate, w_br, layer):
    m, d = h.shape
    nb, bw = w_br.shape[1:3]
    tm = _pick(m, (512, 256))
    tn = 1024
    return pl.pallas_call(
        _merge_kernel, grid=(m // tm, d // tn, nb),
        in_specs=[pl.BlockSpec((tm, d), lambda i, j, r: (i, 0))]
        + [pl.BlockSpec((tm, bw), lambda i, j, r: (i, 0))] * nb
        + [pl.BlockSpec((None, None, d, tn), lambda i, j, r: (layer, r, 0, j)),
           pl.BlockSpec((None, None, bw, tn), lambda i, j, r: (layer, r, 0, j))],
        out_specs=pl.BlockSpec((tm, tn), lambda i, j, r: (i, j)),
        out_shape=jax.ShapeDtypeStruct((m, d), BF16),
        scratch_shapes=[pltpu.VMEM((tm, tn), F32)],
        compiler_params=_cp("arbitrary", "arbitrary", "arbitrary"), name="gated_merge",
    )(h, *ys, w_gate, w_br)


def _outproj_kernel(m_ref, w_ref, s_ref, gl_ref, gc_ref, o_ref, *, tm, rows_per_batch, n_lat_rows):
    out = _dot(m_ref[...], w_ref[...])
    row = pl.program_id(0) * tm + lax.broadcasted_iota(jnp.int32, (tm, 1), 0)
    is_lat = (row % rows_per_batch) < n_lat_rows
    gate = jnp.where(is_lat, gl_ref[...], gc_ref[...])
    o_ref[...] = ALPHA * s_ref[...] + gate * out


def out_projection(m2, w_out, layer, s2, gate_rows, rows_per_batch, n_lat_rows):
    m, d = m2.shape
    tm = _pick(rows_per_batch, (768, 512, 256))
    tn = 1024
    bpb = rows_per_batch // tm
    return pl.pallas_call(
        functools.partial(_outproj_kernel, tm=tm, rows_per_batch=rows_per_batch, n_lat_rows=n_lat_rows),
        grid=(m // tm, d // tn),
        in_specs=[pl.BlockSpec((tm, d), lambda i, j: (i, 0)),
                  pl.BlockSpec((None, d, tn), lambda i, j: (layer, 0, j)),
                  pl.BlockSpec((tm, tn), lambda i, j: (i, j)),
                  pl.BlockSpec((None, 1, tn), lambda i, j: (i // bpb, 0, j)),
                  pl.BlockSpec((None, 1, tn), lambda i, j: (2, 0, j))],
        out_specs=pl.BlockSpec((tm, tn), lambda i, j: (i, j)),
        out_shape=jax.ShapeDtypeStruct((m, d), F32),
        compiler_params=_cp("arbitrary", "arbitrary"), name="out_projection",
    )(m2, w_out, s2, gate_rows, gate_rows)


CONV_ROWS = 256


def _conv_kernel(x_ref, w_ref, b_ref, o_ref, *, t_rows, n_lat_rows, act):
    r = CONV_ROWS
    w0, w1, w2 = w_ref[0:1, :], w_ref[1:2, :], w_ref[2:3, :]
    bias = b_ref[...]
    rid = lax.broadcasted_iota(jnp.int32, (r, 1), 0)

    def body(i, carry):
        r0 = pl.multiple_of(i * r, r)
        cur = x_ref[pl.ds(r0, r), :].astype(F32)
        p0 = pl.multiple_of(jnp.maximum(r0 - 8, 0), 8)
        n0 = pl.multiple_of(jnp.minimum(r0 + r, t_rows - 8), 8)
        prev_row = x_ref[pl.ds(p0, 8), :].astype(F32)[7:8, :]
        next_row = x_ref[pl.ds(n0, 8), :].astype(F32)[0:1, :]
        up = jnp.where(rid == 0, prev_row, pltpu.roll(cur, 1, 0))
        dn = jnp.where(rid == r - 1, next_row, pltpu.roll(cur, r - 1, 0))
        gpos = r0 + rid
        up = jnp.where((gpos == 0) | (gpos == n_lat_rows), 0.0, up)
        dn = jnp.where((gpos == n_lat_rows - 1) | (gpos == t_rows - 1), 0.0, dn)
        y = w0 * up + w1 * cur + w2 * dn + bias
        if act:
            y = _silu(y)
        o_ref[pl.ds(r0, r), :] = y.astype(o_ref.dtype)
        return carry

    lax.fori_loop(0, t_rows // r, body, 0)


def short_conv(p, col0, width, w, b, n_lat_rows, act, out_dtype, name):
    bsz, t, _ = p.shape
    ct = 256
    cb = col0 // ct
    return pl.pallas_call(
        functools.partial(_conv_kernel, t_rows=t, n_lat_rows=n_lat_rows, act=act),
        grid=(bsz, width // ct),
        in_specs=[pl.BlockSpec((None, t, ct), lambda bi, j: (bi, 0, cb + j)),
                  pl.BlockSpec((3, ct), lambda bi, j: (0, j)),
                  pl.BlockSpec((1, ct), lambda bi, j: (0, j))],
        out_specs=pl.BlockSpec((None, t, ct), lambda bi, j: (bi, 0, j)),
        out_shape=jax.ShapeDtypeStruct((bsz, t, width), out_dtype),
        compiler_params=_cp("arbitrary", "arbitrary"), name=name,
    )(p, w, b.reshape(1, width))


def _filter_kernel(w1_ref, b1_ref, w2_ref, b2_ref, w3_ref, b3_ref, fr_ref, w4_ref, dl_ref, o_ref, *, seq, tr):
    n = pl.program_id(0) * tr + lax.broadcasted_iota(jnp.int32, (tr, 1), 0)
    lag = jnp.where(n < seq, n, 2 * seq - n).astype(F32)
    t = lag * (1.0 / seq)
    lane = lax.broadcasted_iota(jnp.int32, (1, LANE), 1)
    band = jnp.where(lane <= HY_BANDS, lane, lane - HY_BANDS).astype(F32)
    arg = (jnp.float32(2.0 * math.pi) * band) * t
    z = jnp.where(lane == 0, t,
                  jnp.where(lane <= HY_BANDS, jnp.cos(arg),
                            jnp.where(lane <= 2 * HY_BANDS, jnp.sin(arg), 0.0)))
    hdn = jnp.sin(fr_ref[0:1, :] * (_dot_hi(z, w1_ref[...]) + b1_ref[...]))
    hdn = jnp.sin(fr_ref[1:2, :] * (_dot_hi(hdn, w2_ref[...]) + b2_ref[...]))
    hdn = jnp.sin(fr_ref[2:3, :] * (_dot_hi(hdn, w3_ref[...]) + b3_ref[...]))
    filt = _dot_hi(hdn, w4_ref[...])
    win = jnp.exp(-t * dl_ref[...])
    win = jnp.where(n == seq, 0.0, win)
    wdt = o_ref.shape[-1]
    o_ref[0] = filt[:, :wdt] * win
    o_ref[1] = filt[:, wdt:] * win


def hyena_filter(seq, w1p, b1, w2, b2, w3, b3, freq, w4d, deltas_abs):
    tr = 256
    wdt = deltas_abs.shape[-1]
    nblk = 2 * seq // tr
    half = seq // tr
    full = lambda shape: pl.BlockSpec(shape, lambda i: (0,) * len(shape))
    return pl.pallas_call(
        functools.partial(_filter_kernel, seq=seq, tr=tr),
        grid=(nblk,),
        in_specs=[full((LANE, HY_HID)), full((1, HY_HID)), full((HY_HID, HY_HID)), full((1, HY_HID)),
                  full((HY_HID, HY_HID)), full((1, HY_HID)), full((3, HY_HID)),
                  pl.BlockSpec((None, HY_HID, 2 * wdt), lambda i: (jnp.where(i < half, 0, 1), 0, 0)),
                  full((1, wdt))],
        out_specs=pl.BlockSpec((2, tr, wdt), lambda i: (0, i, 0)),
        out_shape=jax.ShapeDtypeStruct((2, 2 * seq, wdt), F32),
        compiler_params=_cp("arbitrary"), name="hyena_filter",
    )(w1p, b1, w2, b2, w3, b3, freq, w4d, deltas_abs)


def _dft_num_c(n1):
    return -(-(n1 // 2 + 1) // SUB) * SUB


def _dft_tables(n1):
    n = n1 * DFT_MINOR
    nc = _dft_num_c(n1)
    two_pi = 2.0 * math.pi
    ia = jnp.arange(n1, dtype=jnp.int32)
    ic = jnp.arange(nc, dtype=jnp.int32)
    ib = jnp.arange(DFT_MINOR, dtype=jnp.int32)
    m = (ic[None, :, None] * (DFT_MINOR * ia[None, None, :] + ib[:, None, None])) % n
    ang = m.astype(F32) * (two_pi / n)
    g = jnp.concatenate([jnp.cos(ang), -jnp.sin(ang)], axis=1).astype(BF16)
    th = ((ib[:, None] * ib[None, :]) % DFT_MINOR).astype(F32) * (two_pi / DFT_MINOR)
    c, s = jnp.cos(th), jnp.sin(th)
    mf = jnp.concatenate([jnp.concatenate([c, s], 1), jnp.concatenate([-s, c], 1)], 0).astype(BF16)
    kk = ic[:, None, None] + n1 * ib[None, None, :]
    mi = (ib[None, :, None] * kk) % n
    ps = mi.astype(F32) * (two_pi / n)
    cr, ci = jnp.cos(ps), jnp.sin(ps)
    cinv = jnp.concatenate([jnp.concatenate([cr, -ci], 2), jnp.concatenate([ci, cr], 2)], 1).astype(BF16)
    ph = ((ia[: n1 // 2, None] * ic[None, :]) % n1).astype(F32) * (two_pi / n1)
    wc = jnp.where((ic == 0) | (ic == n1 // 2), 1.0, jnp.where(ic < n1 // 2, 2.0, 0.0)) * (1.0 / n)
    m3 = (jnp.concatenate([jnp.cos(ph), -jnp.sin(ph)], 1) * jnp.tile(wc, 2)[None, :]).astype(BF16)
    return g, mf, cinv, m3


def _s1_kernel(g_ref, x_ref, o_ref):
    for j in range(SUB):
        o_ref[:, j, :] = _dot(g_ref[j], x_ref[:, j, :].astype(BF16))


def dft_stage1(x4, g, comp, rows, wdt, name):
    gsz = x4.shape[0]
    n1x2 = g.shape[1]
    return pl.pallas_call(
        _s1_kernel, grid=(gsz, DFT_MINOR // SUB),
        in_specs=[pl.BlockSpec((SUB, n1x2, rows), lambda i, bb: (bb, 0, 0)),
                  pl.BlockSpec((None, rows, SUB, wdt), lambda i, bb: (i, 0, bb, comp))],
        out_specs=pl.BlockSpec((None, n1x2, SUB, wdt), lambda i, bb: (i, 0, bb, 0)),
        out_shape=jax.ShapeDtypeStruct((gsz, n1x2, DFT_MINOR, wdt), F32),
        compiler_params=_cp("arbitrary", "arbitrary"), name=name,
    )(g, x4)


def _s2f_kernel(mf_ref, a_ref, o_ref):
    wdt = a_ref.shape[-1]
    for j in range(SUB):
        y = _dot(mf_ref[...], a_ref[:, j].astype(BF16).reshape(2 * DFT_MINOR, wdt))
        o_ref[:, j] = y.reshape(2, DFT_MINOR, wdt)


def filter_spectrum(a5, mf):
    no, _, n1, _, wdt = a5.shape
    ct = 512
    spec = pl.BlockSpec((None, 2, SUB, DFT_MINOR, ct), lambda o, c, jc: (o, 0, c, 0, jc))
    return pl.pallas_call(
        _s2f_kernel, grid=(no, n1 // SUB, wdt // ct),
        in_specs=[pl.BlockSpec((2 * DFT_MINOR, 2 * DFT_MINOR), lambda o, c, jc: (0, 0)), spec],
        out_specs=spec,
        out_shape=jax.ShapeDtypeStruct(a5.shape, F32),
        compiler_params=_cp("arbitrary", "arbitrary", "arbitrary"), name="hyena_filter_spectrum",
    )(mf, a5)


def _s2_kernel(mf_ref, ci_ref, h_ref, a_ref, o_ref):
    wdt = a_ref.shape[-1]
    for j in range(SUB):
        y = _dot(mf_ref[...], a_ref[:, j].astype(BF16).reshape(2 * DFT_MINOR, wdt))
        yr, yi = y[:DFT_MINOR], y[DFT_MINOR:]
        hr, hi = h_ref[0, j], h_ref[1, j]
        z = jnp.concatenate([yr * hr - yi * hi, yr * hi + yi * hr], axis=0).astype(BF16)
        o_ref[:, j, :] = _dot(ci_ref[j], z)


def spectral_multiply(a5, spec5, order, mf, cinv):
    bsz, _, n1, _, wdt = a5.shape
    ct = 512
    return pl.pallas_call(
        _s2_kernel, grid=(n1 // SUB, bsz, wdt // ct),
        in_specs=[pl.BlockSpec((2 * DFT_MINOR, 2 * DFT_MINOR), lambda c, bi, jc: (0, 0)),
                  pl.BlockSpec((SUB, 2 * DFT_MINOR, 2 * DFT_MINOR), lambda c, bi, jc: (c, 0, 0)),
                  pl.BlockSpec((None, 2, SUB, DFT_MINOR, ct), lambda c, bi, jc: (order, 0, c, 0, jc)),
                  pl.BlockSpec((None, 2, SUB, DFT_MINOR, ct), lambda c, bi, jc: (bi, 0, c, 0, jc))],
        out_specs=pl.BlockSpec((None, 2 * DFT_MINOR, SUB, ct), lambda c, bi, jc: (bi, 0, c, jc)),
        out_shape=jax.ShapeDtypeStruct((bsz, 2 * DFT_MINOR, n1, wdt), F32),
        compiler_params=_cp("arbitrary", "arbitrary", "arbitrary"), name="hyena_spectral_multiply",
    )(mf, cinv, spec5, a5)


def _s3_mid_kernel(m3_ref, g_ref, bq_ref, z_ref, x_ref, sk_ref, zo_ref, ao_ref):
    wdt = bq_ref.shape[-1]
    for j in range(SUB):
        y = _dot(m3_ref[...], bq_ref[:, j].astype(BF16).reshape(-1, wdt))
        z = x_ref[:, j, :] * (y + z_ref[:, j, :] * sk_ref[...])
        zo_ref[:, j, :] = z
        ao_ref[:, j, :] = _dot(g_ref[j], z.astype(BF16))


def _s3_last_kernel(m3_ref, bq_ref, z_ref, x_ref, sk_ref, gate_ref, o_ref):
    wdt = bq_ref.shape[-1]
    for j in range(SUB):
        y = _dot(m3_ref[...], bq_ref[:, j].astype(BF16).reshape(-1, wdt))
        z = x_ref[:, j, :] * (y + z_ref[:, j, :] * sk_ref[...])
        o_ref[:, j, :] = z * _silu(gate_ref[:, j, :])


def _hyena_ctx_kernel(ff_ref, fd_ref, fi_ref, circ_ref, v_ref, x1_ref, x2_ref, gate_ref, sk_ref, o_ref, *, nc):
    z = v_ref[...].astype(F32)
    xs = (x1_ref, x2_ref)
    for o in range(2):
        hs = _dot(ff_ref[...], circ_ref[o].astype(BF16))
        us = _dot(fd_ref[...], z.astype(BF16))
        hr, hi = hs[:nc], hs[nc:]
        ur, ui = us[:nc], us[nc:]
        zz = jnp.concatenate([ur * hr - ui * hi, ur * hi + ui * hr], axis=0).astype(BF16)
        y = _dot(fi_ref[...], zz)
        z = xs[o][...].astype(F32) * (y + z * sk_ref[o:o + 1, :])
    o_ref[...] = (z * _silu(gate_ref[...].astype(F32))).astype(o_ref.dtype)


def hyena_context(hv, p, circ_c, skip, n_lat_rows, lc):
    bsz, _, w3 = hv.shape
    wdt = w3 // 3
    nc = 2 * lc
    ct = 256
    two_pi = 2.0 * math.pi
    ik = jnp.arange(nc, dtype=jnp.int32)
    th = ((ik[:, None] * ik[None, :]) % nc).astype(F32) * (two_pi / nc)
    c, s = jnp.cos(th), jnp.sin(th)
    ffull = jnp.concatenate([c, -s], axis=0).astype(BF16)
    fdata = ffull[:, :lc]
    finv = (jnp.concatenate([c[:lc], -s[:lc]], axis=1) * (1.0 / nc)).astype(BF16)
    rb = n_lat_rows // lc
    cw = wdt // ct
    full = lambda shape: pl.BlockSpec(shape, lambda bi, j: (0,) * len(shape))
    return pl.pallas_call(
        functools.partial(_hyena_ctx_kernel, nc=nc),
        grid=(bsz, cw),
        in_specs=[full((2 * nc, nc)), full((2 * nc, lc)), full((lc, 2 * nc)),
                  pl.BlockSpec((2, nc, ct), lambda bi, j: (0, 0, j)),
                  pl.BlockSpec((None, lc, ct), lambda bi, j: (bi, rb, j)),
                  pl.BlockSpec((None, lc, ct), lambda bi, j: (bi, rb, cw + j)),
                  pl.BlockSpec((None, lc, ct), lambda bi, j: (bi, rb, 2 * cw + j)),
                  pl.BlockSpec((None, lc, ct), lambda bi, j: (bi, rb, C_HYGATE // ct + j)),
                  pl.BlockSpec((2, ct), lambda bi, j: (0, j))],
        out_specs=pl.BlockSpec((None, lc, ct), lambda bi, j: (bi, 0, j)),
        out_shape=jax.ShapeDtypeStruct((bsz, lc, wdt), BF16),
        compiler_params=_cp("arbitrary", "arbitrary"), name="hyena_context",
    )(ffull, fdata, finv, circ_c, hv, hv, hv, p, skip)


def hyena_latent(hv, p, circ_l, skip, n_lat_rows, tables):
    g, mf, cinv, m3 = tables
    bsz, t, w3 = hv.shape
    wdt = w3 // 3
    n1 = 2 * n_lat_rows // DFT_MINOR
    nc = _dft_num_c(n1)
    half = n1 // 2
    ct = 512
    nj = wdt // ct
    af = dft_stage1(circ_l.reshape(2, n1, DFT_MINOR, wdt), g, 0, n1, wdt, "hyena_filter_stage1")
    spec5 = filter_spectrum(af.reshape(2, 2, nc, DFT_MINOR, wdt), mf)
    gd = g[:, :, :half]
    hv4 = hv.reshape(bsz, t // DFT_MINOR, DFT_MINOR, w3)
    p4 = p.reshape(bsz, t // DFT_MINOR, DFT_MINOR, p.shape[-1])
    sk = skip.reshape(2, 1, wdt)
    a = dft_stage1(hv4, gd, 0, half, wdt, "hyena_stage1")
    bq = spectral_multiply(a.reshape(bsz, 2, nc, DFT_MINOR, wdt), spec5, 0, mf, cinv)
    grid = (bsz, DFT_MINOR // SUB, nj)
    nat = lambda cb: pl.BlockSpec((None, half, SUB, ct), lambda bi, bb, jc: (bi, 0, bb, cb + jc))
    bq_spec = pl.BlockSpec((None, 2, SUB, nc, ct), lambda bi, bb, jc: (bi, 0, bb, 0, jc))
    m3_spec = pl.BlockSpec((half, 2 * nc), lambda bi, bb, jc: (0, 0))
    sk_spec = lambda o: pl.BlockSpec((None, 1, ct), lambda bi, bb, jc: (o, 0, jc))
    z1, a = pl.pallas_call(
        _s3_mid_kernel, grid=grid,
        in_specs=[m3_spec, pl.BlockSpec((SUB, 2 * nc, half), lambda bi, bb, jc: (bb, 0, 0)),
                  bq_spec, nat(0), nat(nj), sk_spec(0)],
        out_specs=[nat(0), pl.BlockSpec((None, 2 * nc, SUB, ct), lambda bi, bb, jc: (bi, 0, bb, jc))],
        out_shape=[jax.ShapeDtypeStruct((bsz, half, DFT_MINOR, wdt), F32),
                   jax.ShapeDtypeStruct((bsz, 2 * nc, DFT_MINOR, wdt), F32)],
        compiler_params=_cp("arbitrary", "arbitrary", "arbitrary"), name="hyena_stage3_mid",
    )(m3, gd, bq.reshape(bsz, 2, DFT_MINOR, nc, wdt), hv4, hv4, sk)
    bq = spectral_multiply(a.reshape(bsz, 2, nc, DFT_MINOR, wdt), spec5, 1, mf, cinv)
    y = pl.pallas_call(
        _s3_last_kernel, grid=grid,
        in_specs=[m3_spec, bq_spec, nat(0), nat(2 * nj), sk_spec(1), nat(C_HYGATE // ct)],
        out_specs=nat(0),
        out_shape=jax.ShapeDtypeStruct((bsz, half, DFT_MINOR, wdt), F32),
        compiler_params=_cp("arbitrary", "arbitrary", "arbitrary"), name="hyena_stage3_last",
    )(m3, bq.reshape(bsz, 2, DFT_MINOR, nc, wdt), z1, hv4, sk, p4)
    return y.reshape(bsz, n_lat_rows, wdt)


SCAN_ROWS = 256


def _scan_block_map(direction, n_lat_rows, t_rows):
    n_lat, n_all = n_lat_rows // SCAN_ROWS, t_rows // SCAN_ROWS
    if direction == 0:
        return lambda t: (t + n_lat) % n_all
    return lambda t: n_all - 1 - t


def _sub_chunks(rows, chunk, direction):
    order = range(rows // chunk) if direction == 0 else reversed(range(rows // chunk))
    return [slice(i * chunk, (i + 1) * chunk) for i in order]


def _tri_mask(n, direction):
    ri = lax.broadcasted_iota(jnp.int32, (n, n), 0)
    ci = lax.broadcasted_iota(jnp.int32, (n, n), 1)
    return (ci <= ri) if direction == 0 else (ci >= ri)


def _expand_heads(v, lane0):
    rows = v.shape[0]
    low_half = lax.broadcasted_iota(jnp.int32, (1, LANE), 1) < SSD_HEADDIM
    tiles = []
    for k in range(SSD_HEADS // 2):
        lo = jnp.broadcast_to(v[:, lane0 + 2 * k:lane0 + 2 * k + 1], (rows, LANE))
        hi = jnp.broadcast_to(v[:, lane0 + 2 * k + 1:lane0 + 2 * k + 2], (rows, LANE))
        tiles.append(jnp.where(low_half, lo, hi))
    return jnp.concatenate(tiles, axis=1)


def _ssd_kernel(*refs, direction):
    d = direction
    if d == 0:
        xs_ref, bm_ref, cm_ref, sm_ref, dtb_ref, alog_ref, o_ref, st_ref = refs
    else:
        (xs_ref, bm_ref, cm_ref, sm_ref, dtb_ref, alog_ref,
         y0_ref, z_ref, dsk_ref, nw_ref, o_ref, st_ref) = refs
    q = SSD_CHUNK
    hd = SSD_HEADDIM
    gw = SSD_RPG * hd

    @pl.when(pl.program_id(1) == 0)
    def _():
        st_ref[...] = jnp.zeros_like(st_ref)

    tri = _tri_mask(q, d)
    trib = jnp.where(tri, 1.0, 0.0).astype(BF16)
    neg_a = -jnp.exp(alog_ref[...])
    for rs in _sub_chunks(xs_ref.shape[0], q, d):
        xs = xs_ref[rs, :].astype(F32)
        bm = bm_ref[rs, :].astype(BF16)
        cm = cm_ref[rs, :].astype(BF16)
        dt = _softplus(sm_ref[rs, :] + dtb_ref[...])
        a = dt * neg_a
        acum = _masked_cumsum(trib, a)
        acum_t = acum.T
        dtx = _expand_heads(dt, SSD_HEADS * d)
        ax = _expand_heads(acum, SSD_HEADS * d)
        atx = ax[q - 1:q, :] if d == 0 else ax[0:1, :]
        xdt = xs * dtx
        e_in = jnp.exp(ax)
        xw = (xdt * jnp.exp(atx - ax)).astype(BF16)
        xdt_b = xdt.astype(BF16)
        ys = []
        for g in range(SSD_GROUPS):
            bg = bm[:, g * SSD_STATE:(g + 1) * SSD_STATE]
            cg = cm[:, g * SSD_STATE:(g + 1) * SSD_STATE]
            sc = _nt(cg, bg)
            st_g = st_ref[g * gw:(g + 1) * gw, :]
            y_int = _nt(cg, st_g.astype(BF16))
            yh, decs = [], []
            for r in range(SSD_RPG):
                hl = SSD_HEADS * d + SSD_RPG * g + r
                seg = acum[:, hl:hl + 1] - acum_t[hl:hl + 1, :]
                dm = jnp.exp(jnp.where(tri, seg, -1e30))
                ch = (SSD_RPG * g + r) * hd
                yh.append(_dot((sc * dm).astype(BF16), xdt_b[:, ch:ch + hd]))
                a_tot = acum[q - 1:q, hl:hl + 1] if d == 0 else acum[0:1, hl:hl + 1]
                decs.append(jnp.broadcast_to(jnp.exp(a_tot), (hd, SSD_STATE)))
            ys.append(jnp.concatenate(yh, axis=1) + y_int * e_in[:, g * gw:(g + 1) * gw])
            st_ref[g * gw:(g + 1) * gw, :] = (st_g * jnp.concatenate(decs, axis=0)
                                              + _tn(xw[:, g * gw:(g + 1) * gw], bg))
        y = jnp.concatenate(ys, axis=1)
        if d == 0:
            o_ref[rs, :] = y
        else:
            y = (y0_ref[rs, :] + y + dsk_ref[...] * xs) * _silu(z_ref[rs, :].astype(F32))
            parts = []
            for g in range(SSD_GROUPS):
                yg = y[:, g * gw:(g + 1) * gw]
                parts.append(yg * lax.rsqrt(jnp.mean(yg * yg, axis=-1, keepdims=True) + EPS))
            o_ref[rs, :] = (jnp.concatenate(parts, axis=1) * nw_ref[...]).astype(o_ref.dtype)


def ssd_mixer(xbc, p, psmall, dtb, alog, dskip, normw, n_lat_rows):
    bsz, t, _ = xbc.shape
    q = SCAN_ROWS
    w = BRANCH_W
    nct = t // q
    outs = None
    for d in range(2):
        cm_ = _scan_block_map(d, n_lat_rows, t)
        row = lambda width, cb, cm_=cm_: pl.BlockSpec((None, q, width), lambda bi, ti: (bi, cm_(ti), cb))
        const = lambda shape: pl.BlockSpec(shape, lambda bi, ti: (0,) * len(shape))
        in_specs = [row(w, 0), row(4 * SSD_STATE, 2), row(4 * SSD_STATE, 3), row(NSMALL, 0),
                    const((1, NSMALL)), const((1, NSMALL))]
        args = [xbc, xbc, xbc, psmall, dtb, alog]
        if d == 1:
            in_specs += [row(w, 0), row(w, C_SSDZ // w), const((1, w)), const((1, w))]
            args += [outs, p, dskip, normw]
        outs = pl.pallas_call(
            functools.partial(_ssd_kernel, direction=d), grid=(bsz, nct),
            in_specs=in_specs, out_specs=row(w, 0),
            out_shape=jax.ShapeDtypeStruct((bsz, t, w), F32 if d == 0 else BF16),
            scratch_shapes=[pltpu.VMEM((SSD_HEADS * SSD_HEADDIM, SSD_STATE), F32)],
            compiler_params=_cp("arbitrary", "arbitrary"), name=f"ssd_scan_dir{d}",
        )(*args)
    return outs


def _gla_kernel(*refs, direction):
    d = direction
    if d == 0:
        q_ref, k_ref, v_ref, sm_ref, w2_ref, b2_ref, o_ref, st_ref = refs
    else:
        q_ref, k_ref, v_ref, sm_ref, w2_ref, b2_ref, y0_ref, g_ref, nw_ref, o_ref, st_ref = refs
    cs = GLA_CHUNK
    dk, dv = GLA_HDK, GLA_HDV

    @pl.when(pl.program_id(1) == 0)
    def _():
        st_ref[...] = jnp.zeros_like(st_ref)

    tri = _tri_mask(cs, d)
    trib = jnp.where(tri, 1.0, 0.0).astype(BF16)
    mid = cs // 2 if d == 0 else cs - 1 - cs // 2
    for rs in _sub_chunks(q_ref.shape[0], cs, d):
        logit = _dot(sm_ref[rs, :].astype(BF16), w2_ref[...]) + b2_ref[...]
        gl = _log_sigmoid(logit) * (1.0 / GLA_NORMALIZER)
        gc = _masked_cumsum(trib, gl)
        g_mid = gc[mid:mid + 1, :]
        g_last = gc[cs - 1:cs, :] if d == 0 else gc[0:1, :]
        qf = q_ref[rs, :].astype(F32) * (dk ** -0.5)
        kf = k_ref[rs, :].astype(F32)
        vb = v_ref[rs, :].astype(BF16)
        qa = (qf * jnp.exp(gc - g_mid)).astype(BF16)
        ka = (kf * jnp.exp(g_mid - gc)).astype(BF16)
        qs = (qf * jnp.exp(gc)).astype(BF16)
        ke = (kf * jnp.exp(g_last - gc)).astype(BF16)
        dec = jnp.exp(g_last)
        ys = []
        for h in range(GLA_HEADS):
            ks, vs = slice(h * dk, (h + 1) * dk), slice(h * dv, (h + 1) * dv)
            att = jnp.where(tri, _nt(qa[:, ks], ka[:, ks]), 0.0)
            st_h = st_ref[h]
            ys.append(_dot(att.astype(BF16), vb[:, vs]) + _nt(qs[:, ks], st_h.astype(BF16)))
            st_ref[h] = st_h * dec[:, ks] + _tn(vb[:, vs], ke[:, ks])
        y = jnp.concatenate(ys, axis=1)
        if d == 0:
            o_ref[rs, :] = y
        else:
            y = y0_ref[rs, :] + y
            parts = []
            for h in range(GLA_HEADS):
                yh = y[:, h * dv:(h + 1) * dv]
                parts.append(yh * lax.rsqrt(jnp.mean(yh * yh, axis=-1, keepdims=True) + EPS))
            y = jnp.concatenate(parts, axis=1) * nw_ref[...]
            o_ref[rs, :] = (y * _silu(g_ref[rs, :].astype(F32))).astype(o_ref.dtype)


def gla_mixer(p, psmall, w2p, b2, normw, n_lat_rows):
    bsz, t, _ = p.shape
    cs = SCAN_ROWS
    w = BRANCH_W
    hw = GLA_HEADS * GLA_HDK
    nct = t // cs
    outs = None
    for d in range(2):
        cm_ = _scan_block_map(d, n_lat_rows, t)
        row = lambda width, cb, cm_=cm_: pl.BlockSpec((None, cs, width), lambda bi, ti: (bi, cm_(ti), cb))
        const = lambda shape: pl.BlockSpec(shape, lambda bi, ti: (0,) * len(shape))
        in_specs = [row(hw, C_GQ // hw), row(hw, C_GK // hw), row(w, C_GV // w), row(NSMALL, 0),
                    pl.BlockSpec((None, NSMALL, hw), lambda bi, ti, d=d: (d, 0, 0)),
                    pl.BlockSpec((None, 1, hw), lambda bi, ti, d=d: (d, 0, 0))]
        args = [p, p, p, psmall, w2p, b2]
        if d == 1:
            in_specs += [row(w, 0), row(w, C_GG // w), const((1, w))]
            args += [outs, p, normw]
        outs = pl.pallas_call(
            functools.partial(_gla_kernel, direction=d), grid=(bsz, nct),
            in_specs=in_specs, out_specs=row(w, 0),
            out_shape=jax.ShapeDtypeStruct((bsz, t, w), F32 if d == 0 else BF16),
            scratch_shapes=[pltpu.VMEM((GLA_HEADS, GLA_HDV, GLA_HDK), F32)],
            compiler_params=_cp("arbitrary", "arbitrary"), name=f"gla_scan_dir{d}",
        )(*args)
    return outs


def _rope(x, cos, sin_signed):
    half = RET_HD // 2
    lo, hi = x[:, :half], x[:, half:]
    lo = lo * cos[:, :half] + pltpu.roll(lo, half // 2, 1) * sin_signed[:, :half]
    hi = hi * cos[:, half:] + pltpu.roll(hi, half // 2, 1) * sin_signed[:, half:]
    return jnp.concatenate([lo, hi], axis=1)


def _ret_kernel(*refs, direction):
    d = direction
    if d == 0:
        q_ref, k_ref, v_ref, cos_ref, sin_ref, dr_ref, o_ref, st_ref = refs
    else:
        q_ref, k_ref, v_ref, cos_ref, sin_ref, dr_ref, y0_ref, g_ref, o_ref, st_ref = refs
    cs = RET_CHUNK
    hdim = RET_HD

    @pl.when(pl.program_id(1) == 0)
    def _():
        st_ref[...] = jnp.zeros_like(st_ref)

    lam_all = -jnp.exp(dr_ref[...])
    tri = _tri_mask(cs, d)
    ri = lax.broadcasted_iota(jnp.int32, (cs, cs), 0)
    ci = lax.broadcasted_iota(jnp.int32, (cs, cs), 1)
    lag = jnp.abs(ri - ci).astype(F32)
    pos = lax.broadcasted_iota(jnp.int32, (cs, hdim), 0).astype(F32)
    steps_in = (pos + 1.0) if d == 0 else (cs - pos)
    steps_out = (cs - 1.0 - pos) if d == 0 else pos
    lams = [lam_all[d:d + 1, h:h + 1] for h in range(RET_HEADS)]
    dms = [jnp.where(tri, jnp.exp(lam * lag), 0.0) for lam in lams]
    w_in = [jnp.exp(lam * steps_in) for lam in lams]
    w_out = [jnp.exp(lam * steps_out) for lam in lams]
    w_chunk = [jnp.exp(lam * cs) for lam in lams]
    for rs in _sub_chunks(q_ref.shape[0], cs, d):
        cos, sin_s = cos_ref[rs, :], sin_ref[rs, :]
        qf = q_ref[rs, :].astype(F32)
        kf = k_ref[rs, :].astype(F32) * (hdim ** -0.5)
        vb = v_ref[rs, :].astype(BF16)
        ys = []
        for h in range(RET_HEADS):
            hs = slice(h * hdim, (h + 1) * hdim)
            qh = _rope(qf[:, hs], cos, sin_s)
            kh = _rope(kf[:, hs], cos, sin_s)
            qb = qh.astype(BF16)
            att = (_nt(qb, kh.astype(BF16)) * dms[h]).astype(BF16)
            st_h = st_ref[h]
            ys.append(_dot(att, vb[:, hs]) + _dot(qb, st_h.astype(BF16)) * w_in[h])
            st_ref[h] = st_h * w_chunk[h] + _tn((kh * w_out[h]).astype(BF16), vb[:, hs])
        y = jnp.concatenate(ys, axis=1)
        if d == 0:
            o_ref[rs, :] = y
        else:
            y = y0_ref[rs, :] + y
            y = jnp.concatenate([_ln_rows(y[:, h * hdim:(h + 1) * hdim]) for h in range(RET_HEADS)], axis=1)
            o_ref[rs, :] = (y * _silu(g_ref[rs, :].astype(F32))).astype(o_ref.dtype)


def ret_mixer(p, cos_t, sin_t, decay_pad, n_lat_rows):
    bsz, t, _ = p.shape
    cs = SCAN_ROWS
    w = BRANCH_W
    nct = t // cs
    outs = None
    for d in range(2):
        cm_ = _scan_block_map(d, n_lat_rows, t)
        row = lambda cb, cm_=cm_: pl.BlockSpec((None, cs, w), lambda bi, ti: (bi, cm_(ti), cb))
        tab = pl.BlockSpec((cs, RET_HD), lambda bi, ti, cm_=cm_: (cm_(ti), 0))
        in_specs = [row(C_RQ // w), row(C_RK // w), row(C_RV // w), tab, tab,
                    pl.BlockSpec((8, LANE), lambda bi, ti: (0, 0))]
        args = [p, p, p, cos_t, sin_t, decay_pad]
        if d == 1:
            in_specs += [row(0), row(C_RG // w)]
            args += [outs, p]
        outs = pl.pallas_call(
            functools.partial(_ret_kernel, direction=d), grid=(bsz, nct),
            in_specs=in_specs, out_specs=row(0),
            out_shape=jax.ShapeDtypeStruct((bsz, t, w), F32 if d == 0 else BF16),
            scratch_shapes=[pltpu.VMEM((RET_HEADS, RET_HD, RET_HD), F32)],
            compiler_params=_cp("arbitrary", "arbitrary"), name=f"ret_scan_dir{d}",
        )(*args)
    return outs


def _rope_tables(n_lat_rows, lc):
    half = RET_HD // 2
    inv = ROPE_BASE ** (-jnp.arange(0, half, 2, dtype=F32) / half)
    tpos = jnp.arange(n_lat_rows)
    row = (tpos // GRID_W).astype(F32)[:, None] * inv[None, :]
    col = (tpos % GRID_W).astype(F32)[:, None] * inv[None, :]
    cos = jnp.concatenate([jnp.cos(row)] * 2 + [jnp.cos(col)] * 2, axis=1)
    sin = jnp.concatenate([-jnp.sin(row), jnp.sin(row), -jnp.sin(col), jnp.sin(col)], axis=1)
    cos = jnp.concatenate([cos, jnp.ones((lc, RET_HD), F32)], axis=0)
    sin = jnp.concatenate([sin, jnp.zeros((lc, RET_HD), F32)], axis=0)
    return cos, sin


def _pad_lanes(v, start):
    v = v.reshape(-1).astype(F32)
    return jnp.zeros((1, NSMALL), F32).at[0, start:start + v.shape[0]].set(v)


def kernel(x, c, ctx, c_ctx, w_ada, b_ada, w_in, hy_conv_w, hy_conv_b, hy_w1, hy_b1, hy_w2, hy_b2, hy_w3,
           hy_b3, hy_w4, hy_freq, hy_skip, ssd_conv_w, ssd_conv_b, ssd_a_log, ssd_dt_bias, ssd_d, ssd_norm_w,
           gla_w2, gla_b2, gla_norm_w, ret_decay, w_gate, w_br, w_out, ln_g, ln_b):
    bsz, n_lat, d = x.shape
    lc = ctx.shape[1]
    t = n_lat + lc
    depth = w_in.shape[0]
    w = BRANCH_W
    assert bsz <= 2 and d == D_MODEL and n_lat % 256 == 0 and lc % 256 == 0

    cs = jnp.zeros((8, d), F32).at[:bsz].set(c).at[2].set(c_ctx)
    mod = ada_modulation(cs, w_ada, b_ada).reshape(depth, 8, 1, 3 * d)

    cos_t, sin_t = _rope_tables(n_lat, lc)
    tables = _dft_tables(2 * n_lat // DFT_MINOR)
    deltas = jnp.abs(jnp.linspace(math.log(HY_TARGET) / HY_SLOW, math.log(HY_TARGET) / HY_FAST, w,
                                  dtype=F32)).reshape(1, w)

    w_gate_b, w_br_b, w_out_b = w_gate.astype(BF16), w_br.astype(BF16), w_out.astype(BF16)
    w_main = jnp.concatenate([w_in[:, :, 0:6144], w_in[:, :, 6176:9248], w_in[:, :, 9280:14400]],
                             axis=2).astype(BF16)
    w_small = jnp.concatenate([w_in[:, :, 6144:6176], w_in[:, :, 9248:9280],
                               jnp.zeros((depth, d, NSMALL - 64), F32)], axis=2).astype(BF16)
    s, h = ln_modulate(x, mod[0], None, None, n_lat, pre_ln=False, emit_h=True, ctx=ctx)
    for l in range(depth):
        h2 = h.reshape(bsz * t, d)
        p = matmul(h2, w_main, l, F32, "in_projection").reshape(bsz, t, NP)
        psmall = matmul(h2, w_small, l, F32, "in_projection_small").reshape(bsz, t, NSMALL)

        hv = short_conv(p, C_HYIN, 3 * w, hy_conv_w[l], hy_conv_b[l], n_lat, False, F32, "hyena_short_conv")
        w1p = jnp.zeros((LANE, HY_HID), F32).at[:2 * HY_BANDS + 1].set(hy_w1[l])
        w4d = hy_w4[l].reshape(HY_HID, 2, 2, w).transpose(2, 0, 1, 3).reshape(2, HY_HID, 2 * w)
        fargs = (w1p, hy_b1[l].reshape(1, -1), hy_w2[l], hy_b2[l].reshape(1, -1), hy_w3[l],
                 hy_b3[l].reshape(1, -1), hy_freq[l], w4d, deltas)
        circ_l = hyena_filter(n_lat, *fargs)
        circ_c = hyena_filter(lc, *fargs)
        y_hy = jnp.concatenate([hyena_latent(hv, p, circ_l, hy_skip[l], n_lat, tables).astype(BF16),
                                hyena_context(hv, p, circ_c, hy_skip[l], n_lat, lc)], axis=1)

        xbc = short_conv(p, C_XBC, 2 * w, ssd_conv_w[l], ssd_conv_b[l], n_lat, True, F32, "ssd_short_conv")
        y_ssd = ssd_mixer(xbc, p, psmall, _pad_lanes(ssd_dt_bias[l], 0), _pad_lanes(ssd_a_log[l], 0),
                          jnp.repeat(ssd_d[l].astype(F32), SSD_HEADDIM).reshape(1, w),
                          ssd_norm_w[l].astype(F32).reshape(1, w), n_lat)

        hw = GLA_HEADS * GLA_HDK
        w2p = jnp.zeros((2, NSMALL, hw), F32)
        for dd in range(2):
            w2p = w2p.at[dd, 32 + GLA_RANK * dd:32 + GLA_RANK * (dd + 1)].set(gla_w2[l, dd])
        y_gla = gla_mixer(p, psmall, w2p.astype(BF16), gla_b2[l].astype(F32).reshape(2, 1, hw),
                          jnp.tile(gla_norm_w[l].astype(F32), GLA_HEADS).reshape(1, w), n_lat)

        decay_pad = jnp.zeros((8, LANE), F32).at[:2, :RET_HEADS].set(ret_decay[l])
        y_ret = ret_mixer(p, cos_t, sin_t, decay_pad, n_lat)

        ys = [y.reshape(bsz * t, w) for y in (y_hy, y_ssd, y_gla, y_ret)]
        m = gated_merge(h2, ys, w_gate_b, w_br_b, l)
        gate_rows = mod[l][:, :, 2 * d:3 * d]
        pre = out_projection(m, w_out_b, l, s.reshape(bsz * t, d), gate_rows, t, n_lat)
        pre = pre.reshape(bsz, t, d)
        if l + 1 < depth:
            s, h = ln_modulate(pre, mod[l + 1], ln_g[l], ln_b[l], n_lat, pre_ln=True, emit_h=True)
        else:
            s = ln_modulate(pre, None, ln_g[l], ln_b[l], n_lat, pre_ln=True, emit_h=False, rows=n_lat)[0]
    return s
```

```python
import functools
import math

import jax
import jax.numpy as jnp
from jax import lax
from jax.experimental import pallas as pl
from jax.experimental.pallas import tpu as pltpu

F32 = jnp.float32
BF16 = jnp.bfloat16
HI = lax.Precision.HIGHEST

D_MODEL = 4096
DEPTH = 2
BRANCH_W = 1024
GRID_W = 64
EPS = 1e-6
ALPHA = (2 * DEPTH) ** 0.25

HY_BANDS = 16
HY_HID = 64
HY_TARGET = 1e-2
HY_FAST = 0.3
HY_SLOW = 1.5

SSD_HEADS = 16
SSD_HEADDIM = 64
SSD_GROUPS = 4
SSD_RPG = 4
SSD_STATE = 128
SSD_CHUNK = 128

GLA_HEADS = 4
GLA_HDK = 128
GLA_HDV = 256
GLA_RANK = 16
GLA_NORMALIZER = 16.0
GLA_CHUNK = 64

RET_HEADS = 4
RET_HD = 256
RET_CHUNK = 128
ROPE_BASE = 10000.0

NP = 14336
C_HYIN, C_HYGATE, C_XBC, C_SSDZ = 0, 3072, 4096, 6144
C_GQ, C_GK, C_GV, C_GG = 7168, 7680, 8192, 9216
C_RQ, C_RK, C_RV, C_RG = 10240, 11264, 12288, 13312
NSMALL = 128
SMALL_W = 2 * SSD_HEADS
W_IN_DT = C_SSDZ
W_IN_LR = C_GG + SMALL_W
LANE = 128
DFT_MINOR = 128
SUB = 8

VMEM_LIMIT = 52 * 1024 * 1024


def _cp(*sem):
    return pltpu.CompilerParams(dimension_semantics=sem, vmem_limit_bytes=VMEM_LIMIT)


def _pick(n, cands):
    for c in cands:
        if n % c == 0:
            return c
    raise ValueError(f"no tile for {n} in {cands}")


def _silu(x):
    return x * jax.nn.sigmoid(x)


def _softplus(x):
    return jnp.maximum(x, 0.0) + jnp.log1p(jnp.exp(-jnp.abs(x)))


def _log_sigmoid(x):
    return jnp.minimum(x, 0.0) - jnp.log1p(jnp.exp(-jnp.abs(x)))


def _nt(a, b):
    return lax.dot_general(a, b, (((1,), (1,)), ((), ())), preferred_element_type=F32)


def _tn(a, b):
    return lax.dot_general(a, b, (((0,), (0,)), ((), ())), preferred_element_type=F32)


def _dot(a, b):
    return jnp.dot(a, b, preferred_element_type=F32)


def _dot_hi(a, b):
    return jnp.dot(a, b, preferred_element_type=F32, precision=HI)


def _masked_cumsum(mask_b, x):
    hi = x.astype(BF16)
    r1 = x - hi.astype(F32)
    mid = r1.astype(BF16)
    lo = (r1 - mid.astype(F32)).astype(BF16)
    return _dot(mask_b, hi) + _dot(mask_b, mid) + _dot(mask_b, lo)


def _ada_kernel(c_ref, w_ref, b_ref, o_ref):
    a = _silu(c_ref[...]).astype(BF16)
    o_ref[...] = _dot(a, w_ref[...].astype(BF16)) + b_ref[...]


def ada_modulation(cs, w_ada, b_ada):
    depth, d, n = w_ada.shape
    tn = 512
    return pl.pallas_call(
        _ada_kernel,
        grid=(depth, n // tn),
        in_specs=[pl.BlockSpec((8, d), lambda l, j: (0, 0)),
                  pl.BlockSpec((None, d, tn), lambda l, j: (l, 0, j)),
                  pl.BlockSpec((None, 1, tn), lambda l, j: (l, 0, j))],
        out_specs=pl.BlockSpec((None, 8, tn), lambda l, j: (l, 0, j)),
        out_shape=jax.ShapeDtypeStruct((depth, 8, n), F32),
        compiler_params=_cp("arbitrary", "arbitrary"),
        name="ada_modulation",
    )(cs, w_ada, b_ada.reshape(depth, 1, n))


def _ln_rows(x):
    xc = x - jnp.mean(x, axis=-1, keepdims=True)
    return xc * lax.rsqrt(jnp.mean(xc * xc, axis=-1, keepdims=True) + EPS)


def _lnmod_kernel(*refs, pre_ln, emit_h, d, n_lat_blocks, split_src):
    it = iter(refs)
    s_ref = next(it)
    c_ref = next(it) if split_src else None
    mod_ref = next(it) if emit_h else None
    g_ref = next(it) if pre_ln else None
    b_ref = next(it) if pre_ln else None
    s_out = next(it) if (pre_ln or split_src) else None
    h_out = next(it) if emit_h else None
    x = s_ref[...]
    if split_src:
        x = jnp.where(pl.program_id(1) < n_lat_blocks, x, c_ref[...])
        s_out[...] = x
    if pre_ln:
        x = _ln_rows(x) * g_ref[...] + b_ref[...]
        s_out[...] = x
    if emit_h:
        shift = mod_ref[:, 0:d]
        scale = mod_ref[:, d:2 * d]
        h_out[...] = (_ln_rows(x) * (1.0 + scale) + shift).astype(BF16)


def ln_modulate(s, mod_rows, ln_g, ln_b, n_lat_rows, *, pre_ln, emit_h, rows=None, ctx=None):
    b, _, d = s.shape
    tr = 256
    nlat = n_lat_rows // tr
    split_src = ctx is not None
    assert not (split_src and pre_ln)
    if split_src:
        t = n_lat_rows + ctx.shape[1]
        args = [s, ctx]
        in_specs = [pl.BlockSpec((None, tr, d), lambda bi, ti: (bi, jnp.minimum(ti, nlat - 1), 0)),
                    pl.BlockSpec((None, tr, d), lambda bi, ti: (bi, jnp.maximum(ti - nlat, 0), 0))]
    else:
        t = s.shape[1] if rows is None else rows
        args = [s]
        in_specs = [pl.BlockSpec((None, tr, d), lambda bi, ti: (bi, ti, 0))]
    if emit_h:
        args.append(mod_rows)
        in_specs.append(pl.BlockSpec((None, 1, 3 * d), lambda bi, ti: (jnp.where(ti < nlat, bi, 2), 0, 0)))
    if pre_ln:
        args += [ln_g.reshape(1, d), ln_b.reshape(1, d)]
        in_specs += [pl.BlockSpec((1, d), lambda bi, ti: (0, 0))] * 2
    out_shape, out_specs = [], []
    if pre_ln or split_src:
        out_shape.append(jax.ShapeDtypeStruct((b, t, d), F32))
        out_specs.append(pl.BlockSpec((None, tr, d), lambda bi, ti: (bi, ti, 0)))
    if emit_h:
        out_shape.append(jax.ShapeDtypeStruct((b, t, d), BF16))
        out_specs.append(pl.BlockSpec((None, tr, d), lambda bi, ti: (bi, ti, 0)))
    return pl.pallas_call(
        functools.partial(_lnmod_kernel, pre_ln=pre_ln, emit_h=emit_h, d=d, n_lat_blocks=nlat,
                          split_src=split_src),
        grid=(b, t // tr), in_specs=in_specs, out_specs=out_specs, out_shape=out_shape,
        compiler_params=_cp("arbitrary", "arbitrary"), name="ln_modulate",
    )(*args)


def _main_projection_row(j, tn):
    col = j * tn
    return col + jnp.where(col >= C_GG, 2 * SMALL_W, jnp.where(col >= C_SSDZ, SMALL_W, 0))


def _mm_nt_kernel(a_ref, w_ref, o_ref):
    o_ref[...] = _nt(a_ref[...], w_ref[0]).astype(o_ref.dtype)


def matmul_nt(a, w_t, layer, out_dtype, name, n_out=None, row_start=None):
    m, k = a.shape
    n = w_t.shape[1] if n_out is None else n_out
    tm = _pick(m, (768, 512, 256))
    tn = _pick(n, (1024, 512, 128))
    if row_start is None:
        row_start = lambda j, tn_: j * tn_
    return pl.pallas_call(
        _mm_nt_kernel, grid=(m // tm, n // tn),
        in_specs=[pl.BlockSpec((tm, k), lambda i, j: (i, 0)),
                  pl.BlockSpec((pl.Element(1), pl.Element(tn), pl.Element(k)),
                               lambda i, j: (layer, pl.multiple_of(row_start(j, tn), 32), 0))],
        out_specs=pl.BlockSpec((tm, tn), lambda i, j: (i, j)),
        out_shape=jax.ShapeDtypeStruct((m, n), out_dtype),
        compiler_params=_cp("arbitrary", "arbitrary"), name=name,
    )(a, w_t)


def _merge_kernel(h_ref, *refs):
    *y_refs, wg_ref, wb_ref, o_ref, acc_ref = refs
    i = pl.program_id(2)
    y = y_refs[-1][...]
    for idx in reversed(range(len(y_refs) - 1)):
        y = jnp.where(i == idx, y_refs[idx][...], y)
    term = jax.nn.sigmoid(_dot(h_ref[...], wg_ref[...])) * _dot(y, wb_ref[...])

    @pl.when(i == 0)
    def _():
        acc_ref[...] = term

    @pl.when(i > 0)
    def _():
        acc_ref[...] += term

    @pl.when(i == pl.num_programs(2) - 1)
    def _():
        o_ref[...] = acc_ref[...].astype(o_ref.dtype)


def gated_merge(h, ys, w_gate, w_br, layer):
    m, d = h.shape
    nb, bw = w_br.shape[1:3]
    tm = _pick(m, (512, 256))
    tn = 1024
    return pl.pallas_call(
        _merge_kernel, grid=(m // tm, d // tn, nb),
        in_specs=[pl.BlockSpec((tm, d), lambda i, j, r: (i, 0))]
        + [pl.BlockSpec((tm, bw), lambda i, j, r: (i, 0))] * nb
        + [pl.BlockSpec((None, None, d, tn), lambda i, j, r: (layer, r, 0, j)),
           pl.BlockSpec((None, None, bw, tn), lambda i, j, r: (layer, r, 0, j))],
        out_specs=pl.BlockSpec((tm, tn), lambda i, j, r: (i, j)),
        out_shape=jax.ShapeDtypeStruct((m, d), BF16),
        scratch_shapes=[pltpu.VMEM((tm, tn), F32)],
        compiler_params=_cp("arbitrary", "arbitrary", "arbitrary"), name="gated_merge",
    )(h, *ys, w_gate, w_br)


def _outproj_kernel(m_ref, w_ref, s_ref, gl_ref, gc_ref, o_ref, *, tm, rows_per_batch, n_lat_rows):
    out = _dot(m_ref[...], w_ref[...])
    row = pl.program_id(0) * tm + lax.broadcasted_iota(jnp.int32, (tm, 1), 0)
    is_lat = (row % rows_per_batch) < n_lat_rows
    gate = jnp.where(is_lat, gl_ref[...], gc_ref[...])
    o_ref[...] = ALPHA * s_ref[...] + gate * out


def out_projection(m2, w_out, layer, s2, gate_rows, rows_per_batch, n_lat_rows):
    m, d = m2.shape
    tm = _pick(rows_per_batch, (768, 512, 256))
    tn = 1024
    bpb = rows_per_batch // tm
    return pl.pallas_call(
        functools.partial(_outproj_kernel, tm=tm, rows_per_batch=rows_per_batch, n_lat_rows=n_lat_rows),
        grid=(m // tm, d // tn),
        in_specs=[pl.BlockSpec((tm, d), lambda i, j: (i, 0)),
                  pl.BlockSpec((None, d, tn), lambda i, j: (layer, 0, j)),
                  pl.BlockSpec((tm, tn), lambda i, j: (i, j)),
                  pl.BlockSpec((None, 1, tn), lambda i, j: (i // bpb, 0, j)),
                  pl.BlockSpec((None, 1, tn), lambda i, j: (2, 0, j))],
        out_specs=pl.BlockSpec((tm, tn), lambda i, j: (i, j)),
        out_shape=jax.ShapeDtypeStruct((m, d), F32),
        compiler_params=_cp("arbitrary", "arbitrary"), name="out_projection",
    )(m2, w_out, s2, gate_rows, gate_rows)


CONV_ROWS = 256


def _conv_kernel(x_ref, w_ref, b_ref, o_ref, *, t_rows, n_lat_rows, act):
    r = CONV_ROWS
    w0, w1, w2 = w_ref[0:1, :], w_ref[1:2, :], w_ref[2:3, :]
    bias = b_ref[...]
    rid = lax.broadcasted_iota(jnp.int32, (r, 1), 0)

    def body(i, carry):
        r0 = pl.multiple_of(i * r, r)
        cur = x_ref[pl.ds(r0, r), :].astype(F32)
        p0 = pl.multiple_of(jnp.maximum(r0 - 8, 0), 8)
        n0 = pl.multiple_of(jnp.minimum(r0 + r, t_rows - 8), 8)
        prev_row = x_ref[pl.ds(p0, 8), :].astype(F32)[7:8, :]
        next_row = x_ref[pl.ds(n0, 8), :].astype(F32)[0:1, :]
        up = jnp.where(rid == 0, prev_row, pltpu.roll(cur, 1, 0))
        dn = jnp.where(rid == r - 1, next_row, pltpu.roll(cur, r - 1, 0))
        gpos = r0 + rid
        up = jnp.where((gpos == 0) | (gpos == n_lat_rows), 0.0, up)
        dn = jnp.where((gpos == n_lat_rows - 1) | (gpos == t_rows - 1), 0.0, dn)
        y = w0 * up + w1 * cur + w2 * dn + bias
        if act:
            y = _silu(y)
        o_ref[pl.ds(r0, r), :] = y.astype(o_ref.dtype)
        return carry

    lax.fori_loop(0, t_rows // r, body, 0)


def short_conv(p, col0, width, w, b, n_lat_rows, act, out_dtype, name):
    bsz, t, _ = p.shape
    ct = 256
    cb = col0 // ct
    return pl.pallas_call(
        functools.partial(_conv_kernel, t_rows=t, n_lat_rows=n_lat_rows, act=act),
        grid=(bsz, width // ct),
        in_specs=[pl.BlockSpec((None, t, ct), lambda bi, j: (bi, 0, cb + j)),
                  pl.BlockSpec((3, ct), lambda bi, j: (0, j)),
                  pl.BlockSpec((1, ct), lambda bi, j: (0, j))],
        out_specs=pl.BlockSpec((None, t, ct), lambda bi, j: (bi, 0, j)),
        out_shape=jax.ShapeDtypeStruct((bsz, t, width), out_dtype),
        compiler_params=_cp("arbitrary", "arbitrary"), name=name,
    )(p, w, b.reshape(1, width))


def _filter_kernel(w1_ref, b1_ref, w2_ref, b2_ref, w3_ref, b3_ref, fr_ref, w4a_ref, w4b_ref, dl_ref, o_ref,
                   *, seq, tr):
    hl = LANE // 2
    n_a = pl.program_id(0) * (2 * tr) + lax.broadcasted_iota(jnp.int32, (tr, 1), 0)
    n_b = n_a + tr
    t_a = jnp.where(n_a < seq, n_a, 2 * seq - n_a).astype(F32) * (1.0 / seq)
    t_b = jnp.where(n_b < seq, n_b, 2 * seq - n_b).astype(F32) * (1.0 / seq)
    lane = lax.broadcasted_iota(jnp.int32, (1, LANE), 1)
    t = jnp.where(lane < hl, t_a, t_b)
    feat = jnp.where(lane < hl, lane, lane - hl)
    band = jnp.where(feat <= HY_BANDS, feat, feat - HY_BANDS).astype(F32)
    arg = (jnp.float32(2.0 * math.pi) * band) * t
    z = jnp.where(feat == 0, t,
                  jnp.where(feat <= HY_BANDS, jnp.cos(arg),
                            jnp.where(feat <= 2 * HY_BANDS, jnp.sin(arg), 0.0)))
    hdn = jnp.sin(fr_ref[0:1, :] * (_dot_hi(z, w1_ref[...]) + b1_ref[...]))
    hdn = jnp.sin(fr_ref[1:2, :] * (_dot_hi(hdn, w2_ref[...]) + b2_ref[...]))
    hdn = jnp.sin(fr_ref[2:3, :] * (_dot_hi(hdn, w3_ref[...]) + b3_ref[...]))
    wdt = o_ref.shape[-1]
    for rows, n_g, t_g, w4_ref in ((slice(0, tr), n_a, t_a, w4a_ref), (slice(tr, 2 * tr), n_b, t_b, w4b_ref)):
        filt = _dot_hi(hdn, w4_ref[...])
        win = jnp.where(n_g == seq, 0.0, jnp.exp(-t_g * dl_ref[...]))
        o_ref[0, rows, :] = filt[:, :wdt] * win
        o_ref[1, rows, :] = filt[:, wdt:] * win


def hyena_filter(seq, w1, b1, w2, b2, w3, b3, freq, w4, deltas_abs):
    tr = 256
    wdt = deltas_abs.shape[-1]
    hid = HY_HID
    hl = LANE // 2
    blockdiag = lambda m: jnp.zeros((LANE, LANE), F32).at[:m.shape[0], :hid].set(m).at[hl:hl + m.shape[0], hid:].set(m)
    twice = lambda v: jnp.tile(v.reshape(-1, hid).astype(F32), (1, 2))
    w4d = w4.astype(F32).reshape(hid, 2, 2, wdt).transpose(2, 0, 1, 3).reshape(2, hid, 2 * wdt)
    zeros = jnp.zeros_like(w4d)
    w4a = jnp.concatenate([w4d, zeros], axis=1)
    w4b = jnp.concatenate([zeros, w4d], axis=1)
    full = lambda shape: pl.BlockSpec(shape, lambda i: (0,) * len(shape))
    direction = lambda row: jnp.where(row >= seq, 1, 0)
    return pl.pallas_call(
        functools.partial(_filter_kernel, seq=seq, tr=tr),
        grid=(seq // tr,),
        in_specs=[full((LANE, LANE)), full((1, LANE)), full((LANE, LANE)), full((1, LANE)),
                  full((LANE, LANE)), full((1, LANE)), full((3, LANE)),
                  pl.BlockSpec((None, LANE, 2 * wdt), lambda i: (direction(2 * tr * i), 0, 0)),
                  pl.BlockSpec((None, LANE, 2 * wdt), lambda i: (direction(2 * tr * i + tr), 0, 0)),
                  full((1, wdt))],
        out_specs=pl.BlockSpec((2, 2 * tr, wdt), lambda i: (0, i, 0)),
        out_shape=jax.ShapeDtypeStruct((2, 2 * seq, wdt), F32),
        compiler_params=_cp("arbitrary"), name="hyena_filter",
    )(blockdiag(w1.astype(F32)), twice(b1), blockdiag(w2.astype(F32)), twice(b2), blockdiag(w3.astype(F32)),
      twice(b3), twice(freq), w4a, w4b, deltas_abs)


def _dft_num_c(n1):
    return -(-(n1 // 2 + 1) // SUB) * SUB


def _dft_tables(n1):
    n = n1 * DFT_MINOR
    nc = _dft_num_c(n1)
    two_pi = 2.0 * math.pi
    ia = jnp.arange(n1, dtype=jnp.int32)
    ic = jnp.arange(nc, dtype=jnp.int32)
    ib = jnp.arange(DFT_MINOR, dtype=jnp.int32)
    m = (ic[None, :, None] * (DFT_MINOR * ia[None, None, :] + ib[:, None, None])) % n
    ang = m.astype(F32) * (two_pi / n)
    g = jnp.concatenate([jnp.cos(ang), -jnp.sin(ang)], axis=1).astype(BF16)
    th = ((ib[:, None] * ib[None, :]) % DFT_MINOR).astype(F32) * (two_pi / DFT_MINOR)
    c, s = jnp.cos(th), jnp.sin(th)
    mf = jnp.concatenate([jnp.concatenate([c, s], 1), jnp.concatenate([-s, c], 1)], 0).astype(BF16)
    kk = ic[:, None, None] + n1 * ib[None, None, :]
    mi = (ib[None, :, None] * kk) % n
    ps = mi.astype(F32) * (two_pi / n)
    cr, ci = jnp.cos(ps), jnp.sin(ps)
    cinv = jnp.concatenate([jnp.concatenate([cr, -ci], 2), jnp.concatenate([ci, cr], 2)], 1).astype(BF16)
    ph = ((ia[: n1 // 2, None] * ic[None, :]) % n1).astype(F32) * (two_pi / n1)
    wc = jnp.where((ic == 0) | (ic == n1 // 2), 1.0, jnp.where(ic < n1 // 2, 2.0, 0.0)) * (1.0 / n)
    m3 = (jnp.concatenate([jnp.cos(ph), -jnp.sin(ph)], 1) * jnp.tile(wc, 2)[None, :]).astype(BF16)
    return g, mf, cinv, m3


def _s1_kernel(g_ref, x_ref, o_ref):
    for j in range(SUB):
        o_ref[:, j, :] = _dot(g_ref[j], x_ref[:, j, :].astype(BF16))


def dft_stage1(x4, g, comp, rows, wdt, name):
    gsz = x4.shape[0]
    n1x2 = g.shape[1]
    return pl.pallas_call(
        _s1_kernel, grid=(gsz, DFT_MINOR // SUB),
        in_specs=[pl.BlockSpec((SUB, n1x2, rows), lambda i, bb: (bb, 0, 0)),
                  pl.BlockSpec((None, rows, SUB, wdt), lambda i, bb: (i, 0, bb, comp))],
        out_specs=pl.BlockSpec((None, n1x2, SUB, wdt), lambda i, bb: (i, 0, bb, 0)),
        out_shape=jax.ShapeDtypeStruct((gsz, n1x2, DFT_MINOR, wdt), F32),
        compiler_params=_cp("arbitrary", "arbitrary"), name=name,
    )(g, x4)


def _s2f_kernel(mf_ref, a_ref, o_ref):
    wdt = a_ref.shape[-1]
    for j in range(SUB):
        y = _dot(mf_ref[...], a_ref[:, j].astype(BF16).reshape(2 * DFT_MINOR, wdt))
        o_ref[:, j] = y.reshape(2, DFT_MINOR, wdt)


def filter_spectrum(a5, mf):
    no, _, n1, _, wdt = a5.shape
    ct = 512
    spec = pl.BlockSpec((None, 2, SUB, DFT_MINOR, ct), lambda o, c, jc: (o, 0, c, 0, jc))
    return pl.pallas_call(
        _s2f_kernel, grid=(no, n1 // SUB, wdt // ct),
        in_specs=[pl.BlockSpec((2 * DFT_MINOR, 2 * DFT_MINOR), lambda o, c, jc: (0, 0)), spec],
        out_specs=spec,
        out_shape=jax.ShapeDtypeStruct(a5.shape, F32),
        compiler_params=_cp("arbitrary", "arbitrary", "arbitrary"), name="hyena_filter_spectrum",
    )(mf, a5)


def _s2_kernel(mf_ref, ci_ref, h_ref, a_ref, o_ref):
    wdt = a_ref.shape[-1]
    for j in range(SUB):
        y = _dot(mf_ref[...], a_ref[:, j].astype(BF16).reshape(2 * DFT_MINOR, wdt))
        yr, yi = y[:DFT_MINOR], y[DFT_MINOR:]
        hr, hi = h_ref[0, j], h_ref[1, j]
        z = jnp.concatenate([yr * hr - yi * hi, yr * hi + yi * hr], axis=0).astype(BF16)
        o_ref[:, j, :] = _dot(ci_ref[j], z)


def spectral_multiply(a5, spec5, order, mf, cinv):
    bsz, _, n1, _, wdt = a5.shape
    ct = 512
    return pl.pallas_call(
        _s2_kernel, grid=(n1 // SUB, bsz, wdt // ct),
        in_specs=[pl.BlockSpec((2 * DFT_MINOR, 2 * DFT_MINOR), lambda c, bi, jc: (0, 0)),
                  pl.BlockSpec((SUB, 2 * DFT_MINOR, 2 * DFT_MINOR), lambda c, bi, jc: (c, 0, 0)),
                  pl.BlockSpec((None, 2, SUB, DFT_MINOR, ct), lambda c, bi, jc: (order, 0, c, 0, jc)),
                  pl.BlockSpec((None, 2, SUB, DFT_MINOR, ct), lambda c, bi, jc: (bi, 0, c, 0, jc))],
        out_specs=pl.BlockSpec((None, 2 * DFT_MINOR, SUB, ct), lambda c, bi, jc: (bi, 0, c, jc)),
        out_shape=jax.ShapeDtypeStruct((bsz, 2 * DFT_MINOR, n1, wdt), F32),
        compiler_params=_cp("arbitrary", "arbitrary", "arbitrary"), name="hyena_spectral_multiply",
    )(mf, cinv, spec5, a5)


def _s3_mid_kernel(m3_ref, g_ref, bq_ref, z_ref, x_ref, sk_ref, zo_ref, ao_ref):
    wdt = bq_ref.shape[-1]
    for j in range(SUB):
        y = _dot(m3_ref[...], bq_ref[:, j].astype(BF16).reshape(-1, wdt))
        z = x_ref[:, j, :] * (y + z_ref[:, j, :] * sk_ref[...])
        zo_ref[:, j, :] = z
        ao_ref[:, j, :] = _dot(g_ref[j], z.astype(BF16))


def _s3_last_kernel(m3_ref, bq_ref, z_ref, x_ref, sk_ref, gate_ref, o_ref):
    wdt = bq_ref.shape[-1]
    for j in range(SUB):
        y = _dot(m3_ref[...], bq_ref[:, j].astype(BF16).reshape(-1, wdt))
        z = x_ref[:, j, :] * (y + z_ref[:, j, :] * sk_ref[...])
        o_ref[:, j, :] = z * _silu(gate_ref[:, j, :])


def _hyena_ctx_kernel(ff_ref, fd_ref, fi_ref, circ_ref, v_ref, x1_ref, x2_ref, gate_ref, sk_ref, o_ref, *, nc):
    z = v_ref[...].astype(F32)
    xs = (x1_ref, x2_ref)
    for o in range(2):
        hs = _dot(ff_ref[...], circ_ref[o].astype(BF16))
        us = _dot(fd_ref[...], z.astype(BF16))
        hr, hi = hs[:nc], hs[nc:]
        ur, ui = us[:nc], us[nc:]
        zz = jnp.concatenate([ur * hr - ui * hi, ur * hi + ui * hr], axis=0).astype(BF16)
        y = _dot(fi_ref[...], zz)
        z = xs[o][...].astype(F32) * (y + z * sk_ref[o:o + 1, :])
    o_ref[...] = (z * _silu(gate_ref[...].astype(F32))).astype(o_ref.dtype)


def hyena_context(hv, p, circ_c, skip, n_lat_rows, lc):
    bsz, _, w3 = hv.shape
    wdt = w3 // 3
    nc = 2 * lc
    ct = 256
    two_pi = 2.0 * math.pi
    ik = jnp.arange(nc, dtype=jnp.int32)
    th = ((ik[:, None] * ik[None, :]) % nc).astype(F32) * (two_pi / nc)
    c, s = jnp.cos(th), jnp.sin(th)
    ffull = jnp.concatenate([c, -s], axis=0).astype(BF16)
    fdata = ffull[:, :lc]
    finv = (jnp.concatenate([c[:lc], -s[:lc]], axis=1) * (1.0 / nc)).astype(BF16)
    rb = n_lat_rows // lc
    cw = wdt // ct
    full = lambda shape: pl.BlockSpec(shape, lambda bi, j: (0,) * len(shape))
    return pl.pallas_call(
        functools.partial(_hyena_ctx_kernel, nc=nc),
        grid=(bsz, cw),
        in_specs=[full((2 * nc, nc)), full((2 * nc, lc)), full((lc, 2 * nc)),
                  pl.BlockSpec((2, nc, ct), lambda bi, j: (0, 0, j)),
                  pl.BlockSpec((None, lc, ct), lambda bi, j: (bi, rb, j)),
                  pl.BlockSpec((None, lc, ct), lambda bi, j: (bi, rb, cw + j)),
                  pl.BlockSpec((None, lc, ct), lambda bi, j: (bi, rb, 2 * cw + j)),
                  pl.BlockSpec((None, lc, ct), lambda bi, j: (bi, rb, C_HYGATE // ct + j)),
                  pl.BlockSpec((2, ct), lambda bi, j: (0, j))],
        out_specs=pl.BlockSpec((None, lc, ct), lambda bi, j: (bi, 0, j)),
        out_shape=jax.ShapeDtypeStruct((bsz, lc, wdt), BF16),
        compiler_params=_cp("arbitrary", "arbitrary"), name="hyena_context",
    )(ffull, fdata, finv, circ_c, hv, hv, hv, p, skip)


def hyena_latent(hv, p, circ_l, skip, n_lat_rows, tables):
    g, mf, cinv, m3 = tables
    bsz, t, w3 = hv.shape
    wdt = w3 // 3
    n1 = 2 * n_lat_rows // DFT_MINOR
    nc = _dft_num_c(n1)
    half = n1 // 2
    ct = 512
    nj = wdt // ct
    af = dft_stage1(circ_l.reshape(2, n1, DFT_MINOR, wdt), g, 0, n1, wdt, "hyena_filter_stage1")
    spec5 = filter_spectrum(af.reshape(2, 2, nc, DFT_MINOR, wdt), mf)
    gd = g[:, :, :half]
    hv4 = hv.reshape(bsz, t // DFT_MINOR, DFT_MINOR, w3)
    p4 = p.reshape(bsz, t // DFT_MINOR, DFT_MINOR, p.shape[-1])
    sk = skip.reshape(2, 1, wdt)
    a = dft_stage1(hv4, gd, 0, half, wdt, "hyena_stage1")
    bq = spectral_multiply(a.reshape(bsz, 2, nc, DFT_MINOR, wdt), spec5, 0, mf, cinv)
    grid = (bsz, DFT_MINOR // SUB, nj)
    nat = lambda cb: pl.BlockSpec((None, half, SUB, ct), lambda bi, bb, jc: (bi, 0, bb, cb + jc))
    bq_spec = pl.BlockSpec((None, 2, SUB, nc, ct), lambda bi, bb, jc: (bi, 0, bb, 0, jc))
    m3_spec = pl.BlockSpec((half, 2 * nc), lambda bi, bb, jc: (0, 0))
    sk_spec = lambda o: pl.BlockSpec((None, 1, ct), lambda bi, bb, jc: (o, 0, jc))
    z1, a = pl.pallas_call(
        _s3_mid_kernel, grid=grid,
        in_specs=[m3_spec, pl.BlockSpec((SUB, 2 * nc, half), lambda bi, bb, jc: (bb, 0, 0)),
                  bq_spec, nat(0), nat(nj), sk_spec(0)],
        out_specs=[nat(0), pl.BlockSpec((None, 2 * nc, SUB, ct), lambda bi, bb, jc: (bi, 0, bb, jc))],
        out_shape=[jax.ShapeDtypeStruct((bsz, half, DFT_MINOR, wdt), F32),
                   jax.ShapeDtypeStruct((bsz, 2 * nc, DFT_MINOR, wdt), F32)],
        compiler_params=_cp("arbitrary", "arbitrary", "arbitrary"), name="hyena_stage3_mid",
    )(m3, gd, bq.reshape(bsz, 2, DFT_MINOR, nc, wdt), hv4, hv4, sk)
    bq = spectral_multiply(a.reshape(bsz, 2, nc, DFT_MINOR, wdt), spec5, 1, mf, cinv)
    y = pl.pallas_call(
        _s3_last_kernel, grid=grid,
        in_specs=[m3_spec, bq_spec, nat(0), nat(2 * nj), sk_spec(1), nat(C_HYGATE // ct)],
        out_specs=nat(0),
        out_shape=jax.ShapeDtypeStruct((bsz, half, DFT_MINOR, wdt), F32),
        compiler_params=_cp("arbitrary", "arbitrary", "arbitrary"), name="hyena_stage3_last",
    )(m3, bq.reshape(bsz, 2, DFT_MINOR, nc, wdt), z1, hv4, sk, p4)
    return y.reshape(bsz, n_lat_rows, wdt)


SCAN_ROWS = 256


def _scan_block_map(direction, n_lat_rows, t_rows):
    n_lat, n_all = n_lat_rows // SCAN_ROWS, t_rows // SCAN_ROWS
    if direction == 0:
        return lambda t: (t + n_lat) % n_all
    return lambda t: n_all - 1 - t


def _sub_chunks(rows, chunk, direction):
    order = range(rows // chunk) if direction == 0 else reversed(range(rows // chunk))
    return [slice(i * chunk, (i + 1) * chunk) for i in order]


def _tri_mask(n, direction):
    ri = lax.broadcasted_iota(jnp.int32, (n, n), 0)
    ci = lax.broadcasted_iota(jnp.int32, (n, n), 1)
    return (ci <= ri) if direction == 0 else (ci >= ri)


def _expand_heads(v, lane0):
    rows = v.shape[0]
    low_half = lax.broadcasted_iota(jnp.int32, (1, LANE), 1) < SSD_HEADDIM
    tiles = []
    for k in range(SSD_HEADS // 2):
        lo = jnp.broadcast_to(v[:, lane0 + 2 * k:lane0 + 2 * k + 1], (rows, LANE))
        hi = jnp.broadcast_to(v[:, lane0 + 2 * k + 1:lane0 + 2 * k + 2], (rows, LANE))
        tiles.append(jnp.where(low_half, lo, hi))
    return jnp.concatenate(tiles, axis=1)


def _ssd_kernel(*refs, direction):
    d = direction
    if d == 0:
        xs_ref, bm_ref, cm_ref, sm_ref, dtb_ref, alog_ref, o_ref, st_ref = refs
    else:
        (xs_ref, bm_ref, cm_ref, sm_ref, dtb_ref, alog_ref,
         y0_ref, z_ref, dsk_ref, nw_ref, o_ref, st_ref) = refs
    q = SSD_CHUNK
    hd = SSD_HEADDIM
    gw = SSD_RPG * hd

    @pl.when(pl.program_id(1) == 0)
    def _():
        st_ref[...] = jnp.zeros_like(st_ref)

    tri = _tri_mask(q, d)
    trib = jnp.where(tri, 1.0, 0.0).astype(BF16)
    neg_a = -jnp.exp(alog_ref[...])
    for rs in _sub_chunks(xs_ref.shape[0], q, d):
        xs = xs_ref[rs, :].astype(F32)
        bm = bm_ref[rs, :].astype(BF16)
        cm = cm_ref[rs, :].astype(BF16)
        dt = _softplus(sm_ref[rs, :] + dtb_ref[...])
        a = dt * neg_a
        acum = _masked_cumsum(trib, a)
        acum_t = acum.T
        dtx = _expand_heads(dt, SSD_HEADS * d)
        ax = _expand_heads(acum, SSD_HEADS * d)
        atx = ax[q - 1:q, :] if d == 0 else ax[0:1, :]
        xdt = xs * dtx
        e_in = jnp.exp(ax)
        xw = (xdt * jnp.exp(atx - ax)).astype(BF16)
        xdt_b = xdt.astype(BF16)
        ys = []
        for g in range(SSD_GROUPS):
            bg = bm[:, g * SSD_STATE:(g + 1) * SSD_STATE]
            cg = cm[:, g * SSD_STATE:(g + 1) * SSD_STATE]
            sc = _nt(cg, bg)
            st_g = st_ref[g * gw:(g + 1) * gw, :]
            y_int = _nt(cg, st_g.astype(BF16))
            yh, decs = [], []
            for r in range(SSD_RPG):
                hl = SSD_HEADS * d + SSD_RPG * g + r
                seg = acum[:, hl:hl + 1] - acum_t[hl:hl + 1, :]
                dm = jnp.exp(jnp.where(tri, seg, -1e30))
                ch = (SSD_RPG * g + r) * hd
                yh.append(_dot((sc * dm).astype(BF16), xdt_b[:, ch:ch + hd]))
                a_tot = acum[q - 1:q, hl:hl + 1] if d == 0 else acum[0:1, hl:hl + 1]
                decs.append(jnp.broadcast_to(jnp.exp(a_tot), (hd, SSD_STATE)))
            ys.append(jnp.concatenate(yh, axis=1) + y_int * e_in[:, g * gw:(g + 1) * gw])
            st_ref[g * gw:(g + 1) * gw, :] = (st_g * jnp.concatenate(decs, axis=0)
                                              + _tn(xw[:, g * gw:(g + 1) * gw], bg))
        y = jnp.concatenate(ys, axis=1)
        if d == 0:
            o_ref[rs, :] = y
        else:
            y = (y0_ref[rs, :] + y + dsk_ref[...] * xs) * _silu(z_ref[rs, :].astype(F32))
            parts = []
            for g in range(SSD_GROUPS):
                yg = y[:, g * gw:(g + 1) * gw]
                parts.append(yg * lax.rsqrt(jnp.mean(yg * yg, axis=-1, keepdims=True) + EPS))
            o_ref[rs, :] = (jnp.concatenate(parts, axis=1) * nw_ref[...]).astype(o_ref.dtype)


def ssd_mixer(xbc, p, psmall, dtb, alog, dskip, normw, n_lat_rows):
    bsz, t, _ = xbc.shape
    q = SCAN_ROWS
    w = BRANCH_W
    nct = t // q
    outs = None
    for d in range(2):
        cm_ = _scan_block_map(d, n_lat_rows, t)
        row = lambda width, cb, cm_=cm_: pl.BlockSpec((None, q, width), lambda bi, ti: (bi, cm_(ti), cb))
        const = lambda shape: pl.BlockSpec(shape, lambda bi, ti: (0,) * len(shape))
        in_specs = [row(w, 0), row(4 * SSD_STATE, 2), row(4 * SSD_STATE, 3), row(NSMALL, 0),
                    const((1, NSMALL)), const((1, NSMALL))]
        args = [xbc, xbc, xbc, psmall, dtb, alog]
        if d == 1:
            in_specs += [row(w, 0), row(w, C_SSDZ // w), const((1, w)), const((1, w))]
            args += [outs, p, dskip, normw]
        outs = pl.pallas_call(
            functools.partial(_ssd_kernel, direction=d), grid=(bsz, nct),
            in_specs=in_specs, out_specs=row(w, 0),
            out_shape=jax.ShapeDtypeStruct((bsz, t, w), F32 if d == 0 else BF16),
            scratch_shapes=[pltpu.VMEM((SSD_HEADS * SSD_HEADDIM, SSD_STATE), F32)],
            compiler_params=_cp("arbitrary", "arbitrary"), name=f"ssd_scan_dir{d}",
        )(*args)
    return outs


def _gla_kernel(*refs, direction):
    d = direction
    if d == 0:
        q_ref, k_ref, v_ref, sm_ref, w2_ref, b2_ref, o_ref, st_ref = refs
    else:
        q_ref, k_ref, v_ref, sm_ref, w2_ref, b2_ref, y0_ref, g_ref, nw_ref, o_ref, st_ref = refs
    cs = GLA_CHUNK
    dk, dv = GLA_HDK, GLA_HDV

    @pl.when(pl.program_id(1) == 0)
    def _():
        st_ref[...] = jnp.zeros_like(st_ref)

    tri = _tri_mask(cs, d)
    trib = jnp.where(tri, 1.0, 0.0).astype(BF16)
    mid = cs // 2 if d == 0 else cs - 1 - cs // 2
    for rs in _sub_chunks(q_ref.shape[0], cs, d):
        logit = _dot(sm_ref[rs, :].astype(BF16), w2_ref[...]) + b2_ref[...]
        gl = _log_sigmoid(logit) * (1.0 / GLA_NORMALIZER)
        gc = _masked_cumsum(trib, gl)
        g_mid = gc[mid:mid + 1, :]
        g_last = gc[cs - 1:cs, :] if d == 0 else gc[0:1, :]
        qf = q_ref[rs, :].astype(F32) * (dk ** -0.5)
        kf = k_ref[rs, :].astype(F32)
        vb = v_ref[rs, :].astype(BF16)
        qa = (qf * jnp.exp(gc - g_mid)).astype(BF16)
        ka = (kf * jnp.exp(g_mid - gc)).astype(BF16)
        qs = (qf * jnp.exp(gc)).astype(BF16)
        ke = (kf * jnp.exp(g_last - gc)).astype(BF16)
        dec = jnp.exp(g_last)
        ys = []
        for h in range(GLA_HEADS):
            ks, vs = slice(h * dk, (h + 1) * dk), slice(h * dv, (h + 1) * dv)
            att = jnp.where(tri, _nt(qa[:, ks], ka[:, ks]), 0.0)
            st_h = st_ref[h]
            ys.append(_dot(att.astype(BF16), vb[:, vs]) + _nt(qs[:, ks], st_h.astype(BF16)))
            st_ref[h] = st_h * dec[:, ks] + _tn(vb[:, vs], ke[:, ks])
        y = jnp.concatenate(ys, axis=1)
        if d == 0:
            o_ref[rs, :] = y
        else:
            y = y0_ref[rs, :] + y
            parts = []
            for h in range(GLA_HEADS):
                yh = y[:, h * dv:(h + 1) * dv]
                parts.append(yh * lax.rsqrt(jnp.mean(yh * yh, axis=-1, keepdims=True) + EPS))
            y = jnp.concatenate(parts, axis=1) * nw_ref[...]
            o_ref[rs, :] = (y * _silu(g_ref[rs, :].astype(F32))).astype(o_ref.dtype)


def gla_mixer(p, psmall, w2p, b2, normw, n_lat_rows):
    bsz, t, _ = p.shape
    cs = SCAN_ROWS
    w = BRANCH_W
    hw = GLA_HEADS * GLA_HDK
    nct = t // cs
    outs = None
    for d in range(2):
        cm_ = _scan_block_map(d, n_lat_rows, t)
        row = lambda width, cb, cm_=cm_: pl.BlockSpec((None, cs, width), lambda bi, ti: (bi, cm_(ti), cb))
        const = lambda shape: pl.BlockSpec(shape, lambda bi, ti: (0,) * len(shape))
        in_specs = [row(hw, C_GQ // hw), row(hw, C_GK // hw), row(w, C_GV // w), row(NSMALL, 0),
                    pl.BlockSpec((None, NSMALL, hw), lambda bi, ti, d=d: (d, 0, 0)),
                    pl.BlockSpec((None, 1, hw), lambda bi, ti, d=d: (d, 0, 0))]
        args = [p, p, p, psmall, w2p, b2]
        if d == 1:
            in_specs += [row(w, 0), row(w, C_GG // w), const((1, w))]
            args += [outs, p, normw]
        outs = pl.pallas_call(
            functools.partial(_gla_kernel, direction=d), grid=(bsz, nct),
            in_specs=in_specs, out_specs=row(w, 0),
            out_shape=jax.ShapeDtypeStruct((bsz, t, w), F32 if d == 0 else BF16),
            scratch_shapes=[pltpu.VMEM((GLA_HEADS, GLA_HDV, GLA_HDK), F32)],
            compiler_params=_cp("arbitrary", "arbitrary"), name=f"gla_scan_dir{d}",
        )(*args)
    return outs


def _rope(x, cos, sin_signed):
    half = RET_HD // 2
    lo, hi = x[:, :half], x[:, half:]
    lo = lo * cos[:, :half] + pltpu.roll(lo, half // 2, 1) * sin_signed[:, :half]
    hi = hi * cos[:, half:] + pltpu.roll(hi, half // 2, 1) * sin_signed[:, half:]
    return jnp.concatenate([lo, hi], axis=1)


def _ret_kernel(*refs, direction):
    d = direction
    if d == 0:
        q_ref, k_ref, v_ref, cos_ref, sin_ref, dr_ref, o_ref, st_ref = refs
    else:
        q_ref, k_ref, v_ref, cos_ref, sin_ref, dr_ref, y0_ref, g_ref, o_ref, st_ref = refs
    cs = RET_CHUNK
    hdim = RET_HD

    @pl.when(pl.program_id(1) == 0)
    def _():
        st_ref[...] = jnp.zeros_like(st_ref)

    lam_all = -jnp.exp(dr_ref[...])
    tri = _tri_mask(cs, d)
    ri = lax.broadcasted_iota(jnp.int32, (cs, cs), 0)
    ci = lax.broadcasted_iota(jnp.int32, (cs, cs), 1)
    lag = jnp.abs(ri - ci).astype(F32)
    pos = lax.broadcasted_iota(jnp.int32, (cs, hdim), 0).astype(F32)
    steps_in = (pos + 1.0) if d == 0 else (cs - pos)
    steps_out = (cs - 1.0 - pos) if d == 0 else pos
    lams = [lam_all[d:d + 1, h:h + 1] for h in range(RET_HEADS)]
    dms = [jnp.where(tri, jnp.exp(lam * lag), 0.0) for lam in lams]
    w_in = [jnp.exp(lam * steps_in) for lam in lams]
    w_out = [jnp.exp(lam * steps_out) for lam in lams]
    w_chunk = [jnp.exp(lam * cs) for lam in lams]
    for rs in _sub_chunks(q_ref.shape[0], cs, d):
        cos, sin_s = cos_ref[rs, :], sin_ref[rs, :]
        for h in range(RET_HEADS):
            hs = slice(h * hdim, (h + 1) * hdim)
            qh = _rope(q_ref[rs, hs].astype(F32), cos, sin_s)
            kh = _rope(k_ref[rs, hs].astype(F32) * (hdim ** -0.5), cos, sin_s)
            vb = v_ref[rs, hs].astype(BF16)
            qb = qh.astype(BF16)
            att = (_nt(qb, kh.astype(BF16)) * dms[h]).astype(BF16)
            st_h = st_ref[h]
            y = _dot(att, vb) + _dot(qb, st_h.astype(BF16)) * w_in[h]
            st_ref[h] = st_h * w_chunk[h] + _tn((kh * w_out[h]).astype(BF16), vb)
            if d == 0:
                o_ref[rs, hs] = y
            else:
                y = _ln_rows(y0_ref[rs, hs] + y)
                o_ref[rs, hs] = (y * _silu(g_ref[rs, hs].astype(F32))).astype(o_ref.dtype)


def ret_mixer(p, cos_t, sin_t, decay_pad, n_lat_rows):
    bsz, t, _ = p.shape
    cs = SCAN_ROWS
    w = BRANCH_W
    nct = t // cs
    outs = None
    for d in range(2):
        cm_ = _scan_block_map(d, n_lat_rows, t)
        row = lambda cb, cm_=cm_: pl.BlockSpec((None, cs, w), lambda bi, ti: (bi, cm_(ti), cb))
        tab = pl.BlockSpec((cs, RET_HD), lambda bi, ti, cm_=cm_: (cm_(ti), 0))
        in_specs = [row(C_RQ // w), row(C_RK // w), row(C_RV // w), tab, tab,
                    pl.BlockSpec((8, LANE), lambda bi, ti: (0, 0))]
        args = [p, p, p, cos_t, sin_t, decay_pad]
        if d == 1:
            in_specs += [row(0), row(C_RG // w)]
            args += [outs, p]
        outs = pl.pallas_call(
            functools.partial(_ret_kernel, direction=d), grid=(bsz, nct),
            in_specs=in_specs, out_specs=row(0),
            out_shape=jax.ShapeDtypeStruct((bsz, t, w), F32 if d == 0 else BF16),
            scratch_shapes=[pltpu.VMEM((RET_HEADS, RET_HD, RET_HD), F32)],
            compiler_params=_cp("arbitrary", "arbitrary"), name=f"ret_scan_dir{d}",
        )(*args)
    return outs


def _rope_tables(n_lat_rows, lc):
    half = RET_HD // 2
    inv = ROPE_BASE ** (-jnp.arange(0, half, 2, dtype=F32) / half)
    tpos = jnp.arange(n_lat_rows)
    row = (tpos // GRID_W).astype(F32)[:, None] * inv[None, :]
    col = (tpos % GRID_W).astype(F32)[:, None] * inv[None, :]
    cos = jnp.concatenate([jnp.cos(row)] * 2 + [jnp.cos(col)] * 2, axis=1)
    sin = jnp.concatenate([-jnp.sin(row), jnp.sin(row), -jnp.sin(col), jnp.sin(col)], axis=1)
    cos = jnp.concatenate([cos, jnp.ones((lc, RET_HD), F32)], axis=0)
    sin = jnp.concatenate([sin, jnp.zeros((lc, RET_HD), F32)], axis=0)
    return cos, sin


def _pad_lanes(v, start):
    v = v.reshape(-1).astype(F32)
    return jnp.zeros((1, NSMALL), F32).at[0, start:start + v.shape[0]].set(v)


def kernel(x, c, ctx, c_ctx, w_ada, b_ada, w_in, hy_conv_w, hy_conv_b, hy_w1, hy_b1, hy_w2, hy_b2, hy_w3,
           hy_b3, hy_w4, hy_freq, hy_skip, ssd_conv_w, ssd_conv_b, ssd_a_log, ssd_dt_bias, ssd_d, ssd_norm_w,
           gla_w2, gla_b2, gla_norm_w, ret_decay, w_gate, w_br, w_out, ln_g, ln_b):
    bsz, n_lat, d = x.shape
    lc = ctx.shape[1]
    t = n_lat + lc
    depth = w_in.shape[0]
    w = BRANCH_W
    assert bsz <= 2 and d == D_MODEL and n_lat % 256 == 0 and lc % 256 == 0

    cs = jnp.zeros((8, d), F32).at[:bsz].set(c).at[2].set(c_ctx)
    mod = ada_modulation(cs, w_ada, b_ada).reshape(depth, 8, 1, 3 * d)

    cos_t, sin_t = _rope_tables(n_lat, lc)
    tables = _dft_tables(2 * n_lat // DFT_MINOR)
    deltas = jnp.abs(jnp.linspace(math.log(HY_TARGET) / HY_SLOW, math.log(HY_TARGET) / HY_FAST, w,
                                  dtype=F32)).reshape(1, w)

    w_gate_b, w_br_b, w_out_b = w_gate.astype(BF16), w_br.astype(BF16), w_out.astype(BF16)
    w_in_t = jnp.swapaxes(w_in, 1, 2).astype(BF16)
    w_small = jnp.concatenate([w_in_t[:, W_IN_DT:W_IN_DT + SMALL_W], w_in_t[:, W_IN_LR:W_IN_LR + SMALL_W],
                               jnp.zeros((depth, NSMALL - 64, d), BF16)], axis=1)
    s, h = ln_modulate(x, mod[0], None, None, n_lat, pre_ln=False, emit_h=True, ctx=ctx)
    for l in range(depth):
        h2 = h.reshape(bsz * t, d)
        p = matmul_nt(h2, w_in_t, l, F32, "in_projection", n_out=NP,
                      row_start=_main_projection_row).reshape(bsz, t, NP)
        psmall = matmul_nt(h2, w_small, l, F32, "in_projection_small").reshape(bsz, t, NSMALL)

        hv = short_conv(p, C_HYIN, 3 * w, hy_conv_w[l], hy_conv_b[l], n_lat, False, F32, "hyena_short_conv")
        fargs = (hy_w1[l], hy_b1[l], hy_w2[l], hy_b2[l], hy_w3[l], hy_b3[l], hy_freq[l], hy_w4[l], deltas)
        circ_l = hyena_filter(n_lat, *fargs)
        circ_c = hyena_filter(lc, *fargs)
        y_hy = jnp.concatenate([hyena_latent(hv, p, circ_l, hy_skip[l], n_lat, tables).astype(BF16),
                                hyena_context(hv, p, circ_c, hy_skip[l], n_lat, lc)], axis=1)

        xbc = short_conv(p, C_XBC, 2 * w, ssd_conv_w[l], ssd_conv_b[l], n_lat, True, F32, "ssd_short_conv")
        y_ssd = ssd_mixer(xbc, p, psmall, _pad_lanes(ssd_dt_bias[l], 0), _pad_lanes(ssd_a_log[l], 0),
                          jnp.repeat(ssd_d[l].astype(F32), SSD_HEADDIM).reshape(1, w),
                          ssd_norm_w[l].astype(F32).reshape(1, w), n_lat)

        hw = GLA_HEADS * GLA_HDK
        w2p = jnp.zeros((2, NSMALL, hw), F32)
        for dd in range(2):
            w2p = w2p.at[dd, SMALL_W + GLA_RANK * dd:SMALL_W + GLA_RANK * (dd + 1)].set(gla_w2[l, dd])
        y_gla = gla_mixer(p, psmall, w2p.astype(BF16), gla_b2[l].astype(F32).reshape(2, 1, hw),
                          jnp.tile(gla_norm_w[l].astype(F32), GLA_HEADS).reshape(1, w), n_lat)

        decay_pad = jnp.zeros((8, LANE), F32).at[:2, :RET_HEADS].set(ret_decay[l])
        y_ret = ret_mixer(p, cos_t, sin_t, decay_pad, n_lat)

        ys = [y.reshape(bsz * t, w) for y in (y_hy, y_ssd, y_gla, y_ret)]
        m = gated_merge(h2, ys, w_gate_b, w_br_b, l)
        gate_rows = mod[l][:, :, 2 * d:3 * d]
        pre = out_projection(m, w_out_b, l, s.reshape(bsz * t, d), gate_rows, t, n_lat)
        pre = pre.reshape(bsz, t, d)
        if l + 1 < depth:
            s, h = ln_modulate(pre, mod[l + 1], ln_g[l], ln_b[l], n_lat, pre_ln=True, emit_h=True)
        else:
            s = ln_modulate(pre, None, ln_g[l], ln_b[l], n_lat, pre_ln=True, emit_h=False, rows=n_lat)[0]
    return s
```

```python
import functools
import math

import jax
import jax.numpy as jnp
from jax import lax
from jax.experimental import pallas as pl
from jax.experimental.pallas import tpu as pltpu

F32 = jnp.float32
BF16 = jnp.bfloat16
HI = lax.Precision.HIGHEST

D_MODEL = 4096
DEPTH = 2
BRANCH_W = 1024
GRID_W = 64
EPS = 1e-6
ALPHA = (2 * DEPTH) ** 0.25

HY_BANDS = 16
HY_HID = 64
HY_TARGET = 1e-2
HY_FAST = 0.3
HY_SLOW = 1.5

SSD_HEADS = 16
SSD_HEADDIM = 64
SSD_GROUPS = 4
SSD_RPG = 4
SSD_STATE = 128
SSD_CHUNK = 128

GLA_HEADS = 4
GLA_HDK = 128
GLA_HDV = 256
GLA_RANK = 16
GLA_NORMALIZER = 16.0
GLA_CHUNK = 64

RET_HEADS = 4
RET_HD = 256
RET_CHUNK = 128
ROPE_BASE = 10000.0

NP = 14336
C_HYIN, C_HYGATE, C_XBC, C_SSDZ = 0, 3072, 4096, 6144
C_GQ, C_GK, C_GV, C_GG = 7168, 7680, 8192, 9216
C_RQ, C_RK, C_RV, C_RG = 10240, 11264, 12288, 13312
NSMALL = 128
SMALL_W = 2 * SSD_HEADS
W_IN_DT = C_SSDZ
W_IN_LR = C_GG + SMALL_W
LANE = 128
DFT_MINOR = 128
SUB = 8

VMEM_LIMIT = 52 * 1024 * 1024


def _cp(*sem):
    return pltpu.CompilerParams(dimension_semantics=sem, vmem_limit_bytes=VMEM_LIMIT)


def _pick(n, cands):
    for c in cands:
        if n % c == 0:
            return c
    raise ValueError(f"no tile for {n} in {cands}")


def _silu(x):
    return x * jax.nn.sigmoid(x)


def _softplus(x):
    return jnp.maximum(x, 0.0) + jnp.log1p(jnp.exp(-jnp.abs(x)))


def _log_sigmoid(x):
    return jnp.minimum(x, 0.0) - jnp.log1p(jnp.exp(-jnp.abs(x)))


def _nt(a, b):
    return lax.dot_general(a, b, (((1,), (1,)), ((), ())), preferred_element_type=F32)


def _tn(a, b):
    return lax.dot_general(a, b, (((0,), (0,)), ((), ())), preferred_element_type=F32)


def _dot(a, b):
    return jnp.dot(a, b, preferred_element_type=F32)


def _dot_hi(a, b):
    return jnp.dot(a, b, preferred_element_type=F32, precision=HI)


def _masked_cumsum(mask_b, x):
    hi = x.astype(BF16)
    r1 = x - hi.astype(F32)
    mid = r1.astype(BF16)
    lo = (r1 - mid.astype(F32)).astype(BF16)
    return _dot(mask_b, hi) + _dot(mask_b, mid) + _dot(mask_b, lo)


def _ada_kernel(c_ref, w_ref, b_ref, o_ref):
    a = _silu(c_ref[...]).astype(BF16)
    o_ref[...] = _dot(a, w_ref[...].astype(BF16)) + b_ref[...]


def ada_modulation(cs, w_ada, b_ada):
    depth, d, n = w_ada.shape
    tn = 512
    return pl.pallas_call(
        _ada_kernel,
        grid=(depth, n // tn),
        in_specs=[pl.BlockSpec((8, d), lambda l, j: (0, 0)),
                  pl.BlockSpec((None, d, tn), lambda l, j: (l, 0, j)),
                  pl.BlockSpec((None, 1, tn), lambda l, j: (l, 0, j))],
        out_specs=pl.BlockSpec((None, 8, tn), lambda l, j: (l, 0, j)),
        out_shape=jax.ShapeDtypeStruct((depth, 8, n), F32),
        compiler_params=_cp("arbitrary", "arbitrary"),
        name="ada_modulation",
    )(cs, w_ada, b_ada.reshape(depth, 1, n))


def _ln_rows(x):
    xc = x - jnp.mean(x, axis=-1, keepdims=True)
    return xc * lax.rsqrt(jnp.mean(xc * xc, axis=-1, keepdims=True) + EPS)


def _lnmod_kernel(*refs, pre_ln, emit_h, d, n_lat_blocks, split_src):
    it = iter(refs)
    s_ref = next(it)
    c_ref = next(it) if split_src else None
    mod_ref = next(it) if emit_h else None
    g_ref = next(it) if pre_ln else None
    b_ref = next(it) if pre_ln else None
    s_out = next(it) if (pre_ln or split_src) else None
    h_out = next(it) if emit_h else None
    x = s_ref[...]
    if split_src:
        x = jnp.where(pl.program_id(1) < n_lat_blocks, x, c_ref[...])
        s_out[...] = x
    if pre_ln:
        x = _ln_rows(x) * g_ref[...] + b_ref[...]
        s_out[...] = x
    if emit_h:
        shift = mod_ref[:, 0:d]
        scale = mod_ref[:, d:2 * d]
        h_out[...] = (_ln_rows(x) * (1.0 + scale) + shift).astype(BF16)


def ln_modulate(s, mod_rows, ln_g, ln_b, n_lat_rows, *, pre_ln, emit_h, rows=None, ctx=None):
    b, _, d = s.shape
    tr = 256
    nlat = n_lat_rows // tr
    split_src = ctx is not None
    assert not (split_src and pre_ln)
    if split_src:
        t = n_lat_rows + ctx.shape[1]
        args = [s, ctx]
        in_specs = [pl.BlockSpec((None, tr, d), lambda bi, ti: (bi, jnp.minimum(ti, nlat - 1), 0)),
                    pl.BlockSpec((None, tr, d), lambda bi, ti: (bi, jnp.maximum(ti - nlat, 0), 0))]
    else:
        t = s.shape[1] if rows is None else rows
        args = [s]
        in_specs = [pl.BlockSpec((None, tr, d), lambda bi, ti: (bi, ti, 0))]
    if emit_h:
        args.append(mod_rows)
        in_specs.append(pl.BlockSpec((None, 1, 3 * d), lambda bi, ti: (jnp.where(ti < nlat, bi, 2), 0, 0)))
    if pre_ln:
        args += [ln_g.reshape(1, d), ln_b.reshape(1, d)]
        in_specs += [pl.BlockSpec((1, d), lambda bi, ti: (0, 0))] * 2
    out_shape, out_specs = [], []
    if pre_ln or split_src:
        out_shape.append(jax.ShapeDtypeStruct((b, t, d), F32))
        out_specs.append(pl.BlockSpec((None, tr, d), lambda bi, ti: (bi, ti, 0)))
    if emit_h:
        out_shape.append(jax.ShapeDtypeStruct((b, t, d), BF16))
        out_specs.append(pl.BlockSpec((None, tr, d), lambda bi, ti: (bi, ti, 0)))
    return pl.pallas_call(
        functools.partial(_lnmod_kernel, pre_ln=pre_ln, emit_h=emit_h, d=d, n_lat_blocks=nlat,
                          split_src=split_src),
        grid=(b, t // tr), in_specs=in_specs, out_specs=out_specs, out_shape=out_shape,
        compiler_params=_cp("arbitrary", "arbitrary"), name="ln_modulate",
    )(*args)


def _main_projection_row(j, tn):
    col = j * tn
    return col + jnp.where(col >= C_GG, 2 * SMALL_W, jnp.where(col >= C_SSDZ, SMALL_W, 0))


def _mm_nt_kernel(a_ref, w_ref, o_ref):
    o_ref[...] = _nt(a_ref[...], w_ref[0]).astype(o_ref.dtype)


def matmul_nt(a, w_t, layer, out_dtype, name, n_out=None, row_start=None):
    m, k = a.shape
    n = w_t.shape[1] if n_out is None else n_out
    tm = _pick(m, (1056, 768, 512, 256))
    tn = _pick(n, (1024, 512, 128))
    if row_start is None:
        row_start = lambda j, tn_: j * tn_
    return pl.pallas_call(
        _mm_nt_kernel, grid=(m // tm, n // tn),
        in_specs=[pl.BlockSpec((tm, k), lambda i, j: (i, 0)),
                  pl.BlockSpec((pl.Element(1), pl.Element(tn), pl.Element(k)),
                               lambda i, j: (layer, pl.multiple_of(row_start(j, tn), 32), 0))],
        out_specs=pl.BlockSpec((tm, tn), lambda i, j: (i, j)),
        out_shape=jax.ShapeDtypeStruct((m, n), out_dtype),
        compiler_params=_cp("arbitrary", "arbitrary"), name=name,
    )(a, w_t)


def _merge_kernel(h_ref, *refs):
    *y_refs, wg_ref, wb_ref, o_ref, acc_ref = refs
    i = pl.program_id(2)
    y = y_refs[-1][...]
    for idx in reversed(range(len(y_refs) - 1)):
        y = jnp.where(i == idx, y_refs[idx][...], y)
    term = jax.nn.sigmoid(_dot(h_ref[...], wg_ref[...])) * _dot(y, wb_ref[...])

    @pl.when(i == 0)
    def _():
        acc_ref[...] = term

    @pl.when(i > 0)
    def _():
        acc_ref[...] += term

    @pl.when(i == pl.num_programs(2) - 1)
    def _():
        o_ref[...] = acc_ref[...].astype(o_ref.dtype)


def gated_merge(h, ys, w_gate, w_br, layer):
    m, d = h.shape
    nb, bw = w_br.shape[1:3]
    tm = _pick(m, (512, 256))
    tn = 1024
    return pl.pallas_call(
        _merge_kernel, grid=(m // tm, d // tn, nb),
        in_specs=[pl.BlockSpec((tm, d), lambda i, j, r: (i, 0))]
        + [pl.BlockSpec((tm, bw), lambda i, j, r: (i, 0))] * nb
        + [pl.BlockSpec((None, None, d, tn), lambda i, j, r: (layer, r, 0, j)),
           pl.BlockSpec((None, None, bw, tn), lambda i, j, r: (layer, r, 0, j))],
        out_specs=pl.BlockSpec((tm, tn), lambda i, j, r: (i, j)),
        out_shape=jax.ShapeDtypeStruct((m, d), BF16),
        scratch_shapes=[pltpu.VMEM((tm, tn), F32)],
        compiler_params=_cp("arbitrary", "arbitrary", "arbitrary"), name="gated_merge",
    )(h, *ys, w_gate, w_br)


def _outproj_kernel(m_ref, w_ref, s_ref, gl_ref, gc_ref, o_ref, *, tm, rows_per_batch, n_lat_rows):
    out = _dot(m_ref[...], w_ref[...])
    row = pl.program_id(0) * tm + lax.broadcasted_iota(jnp.int32, (tm, 1), 0)
    is_lat = (row % rows_per_batch) < n_lat_rows
    gate = jnp.where(is_lat, gl_ref[...], gc_ref[...])
    o_ref[...] = ALPHA * s_ref[...] + gate * out


def out_projection(m2, w_out, layer, s2, gate_rows, rows_per_batch, n_lat_rows):
    m, d = m2.shape
    tm = _pick(rows_per_batch, (768, 512, 256))
    tn = 1024
    bpb = rows_per_batch // tm
    return pl.pallas_call(
        functools.partial(_outproj_kernel, tm=tm, rows_per_batch=rows_per_batch, n_lat_rows=n_lat_rows),
        grid=(m // tm, d // tn),
        in_specs=[pl.BlockSpec((tm, d), lambda i, j: (i, 0)),
                  pl.BlockSpec((None, d, tn), lambda i, j: (layer, 0, j)),
                  pl.BlockSpec((tm, tn), lambda i, j: (i, j)),
                  pl.BlockSpec((None, 1, tn), lambda i, j: (i // bpb, 0, j)),
                  pl.BlockSpec((None, 1, tn), lambda i, j: (2, 0, j))],
        out_specs=pl.BlockSpec((tm, tn), lambda i, j: (i, j)),
        out_shape=jax.ShapeDtypeStruct((m, d), F32),
        compiler_params=_cp("arbitrary", "arbitrary"), name="out_projection",
    )(m2, w_out, s2, gate_rows, gate_rows)


CONV_ROWS = 256


def _conv_kernel(x_ref, w_ref, b_ref, o_ref, *, t_rows, n_lat_rows, act):
    r = CONV_ROWS
    w0, w1, w2 = w_ref[0:1, :], w_ref[1:2, :], w_ref[2:3, :]
    bias = b_ref[...]
    rid = lax.broadcasted_iota(jnp.int32, (r, 1), 0)

    def body(i, carry):
        r0 = pl.multiple_of(i * r, r)
        cur = x_ref[pl.ds(r0, r), :].astype(F32)
        p0 = pl.multiple_of(jnp.maximum(r0 - 8, 0), 8)
        n0 = pl.multiple_of(jnp.minimum(r0 + r, t_rows - 8), 8)
        prev_row = x_ref[pl.ds(p0, 8), :].astype(F32)[7:8, :]
        next_row = x_ref[pl.ds(n0, 8), :].astype(F32)[0:1, :]
        up = jnp.where(rid == 0, prev_row, pltpu.roll(cur, 1, 0))
        dn = jnp.where(rid == r - 1, next_row, pltpu.roll(cur, r - 1, 0))
        gpos = r0 + rid
        up = jnp.where((gpos == 0) | (gpos == n_lat_rows), 0.0, up)
        dn = jnp.where((gpos == n_lat_rows - 1) | (gpos == t_rows - 1), 0.0, dn)
        y = w0 * up + w1 * cur + w2 * dn + bias
        if act:
            y = _silu(y)
        o_ref[pl.ds(r0, r), :] = y.astype(o_ref.dtype)
        return carry

    lax.fori_loop(0, t_rows // r, body, 0)


def short_conv(p, col0, width, w, b, n_lat_rows, act, out_dtype, name):
    bsz, t, _ = p.shape
    ct = 256
    cb = col0 // ct
    return pl.pallas_call(
        functools.partial(_conv_kernel, t_rows=t, n_lat_rows=n_lat_rows, act=act),
        grid=(bsz, width // ct),
        in_specs=[pl.BlockSpec((None, t, ct), lambda bi, j: (bi, 0, cb + j)),
                  pl.BlockSpec((3, ct), lambda bi, j: (0, j)),
                  pl.BlockSpec((1, ct), lambda bi, j: (0, j))],
        out_specs=pl.BlockSpec((None, t, ct), lambda bi, j: (bi, 0, j)),
        out_shape=jax.ShapeDtypeStruct((bsz, t, width), out_dtype),
        compiler_params=_cp("arbitrary", "arbitrary"), name=name,
    )(p, w, b.reshape(1, width))


def _filter_kernel(w1_ref, b1_ref, w2_ref, b2_ref, w3_ref, b3_ref, fr_ref, w4a_ref, w4b_ref, dl_ref, o_ref,
                   *, seq, tr):
    hl = LANE // 2
    n_a = pl.program_id(0) * (2 * tr) + lax.broadcasted_iota(jnp.int32, (tr, 1), 0)
    n_b = n_a + tr
    t_a = jnp.where(n_a < seq, n_a, 2 * seq - n_a).astype(F32) * (1.0 / seq)
    t_b = jnp.where(n_b < seq, n_b, 2 * seq - n_b).astype(F32) * (1.0 / seq)
    lane = lax.broadcasted_iota(jnp.int32, (1, LANE), 1)
    t = jnp.where(lane < hl, t_a, t_b)
    feat = jnp.where(lane < hl, lane, lane - hl)
    band = jnp.where(feat <= HY_BANDS, feat, feat - HY_BANDS).astype(F32)
    arg = (jnp.float32(2.0 * math.pi) * band) * t
    z = jnp.where(feat == 0, t,
                  jnp.where(feat <= HY_BANDS, jnp.cos(arg),
                            jnp.where(feat <= 2 * HY_BANDS, jnp.sin(arg), 0.0)))
    hdn = jnp.sin(fr_ref[0:1, :] * (_dot_hi(z, w1_ref[...]) + b1_ref[...]))
    hdn = jnp.sin(fr_ref[1:2, :] * (_dot_hi(hdn, w2_ref[...]) + b2_ref[...]))
    hdn = jnp.sin(fr_ref[2:3, :] * (_dot_hi(hdn, w3_ref[...]) + b3_ref[...]))
    wdt = o_ref.shape[-1]
    for rows, n_g, t_g, w4_ref in ((slice(0, tr), n_a, t_a, w4a_ref), (slice(tr, 2 * tr), n_b, t_b, w4b_ref)):
        filt = _dot_hi(hdn, w4_ref[...])
        win = jnp.where(n_g == seq, 0.0, jnp.exp(-t_g * dl_ref[...]))
        o_ref[0, rows, :] = filt[:, :wdt] * win
        o_ref[1, rows, :] = filt[:, wdt:] * win


def hyena_filter(seq, w1, b1, w2, b2, w3, b3, freq, w4, deltas_abs):
    tr = 256
    wdt = deltas_abs.shape[-1]
    hid = HY_HID
    hl = LANE // 2
    blockdiag = lambda m: jnp.zeros((LANE, LANE), F32).at[:m.shape[0], :hid].set(m).at[hl:hl + m.shape[0], hid:].set(m)
    twice = lambda v: jnp.tile(v.reshape(-1, hid).astype(F32), (1, 2))
    w4d = w4.astype(F32).reshape(hid, 2, 2, wdt).transpose(2, 0, 1, 3).reshape(2, hid, 2 * wdt)
    zeros = jnp.zeros_like(w4d)
    w4a = jnp.concatenate([w4d, zeros], axis=1)
    w4b = jnp.concatenate([zeros, w4d], axis=1)
    full = lambda shape: pl.BlockSpec(shape, lambda i: (0,) * len(shape))
    direction = lambda row: jnp.where(row >= seq, 1, 0)
    return pl.pallas_call(
        functools.partial(_filter_kernel, seq=seq, tr=tr),
        grid=(seq // tr,),
        in_specs=[full((LANE, LANE)), full((1, LANE)), full((LANE, LANE)), full((1, LANE)),
                  full((LANE, LANE)), full((1, LANE)), full((3, LANE)),
                  pl.BlockSpec((None, LANE, 2 * wdt), lambda i: (direction(2 * tr * i), 0, 0)),
                  pl.BlockSpec((None, LANE, 2 * wdt), lambda i: (direction(2 * tr * i + tr), 0, 0)),
                  full((1, wdt))],
        out_specs=pl.BlockSpec((2, 2 * tr, wdt), lambda i: (0, i, 0)),
        out_shape=jax.ShapeDtypeStruct((2, 2 * seq, wdt), F32),
        compiler_params=_cp("arbitrary"), name="hyena_filter",
    )(blockdiag(w1.astype(F32)), twice(b1), blockdiag(w2.astype(F32)), twice(b2), blockdiag(w3.astype(F32)),
      twice(b3), twice(freq), w4a, w4b, deltas_abs)


def _dft_num_c(n1):
    return -(-(n1 // 2 + 1) // SUB) * SUB


def _dft_tables(n1):
    n = n1 * DFT_MINOR
    nc = _dft_num_c(n1)
    two_pi = 2.0 * math.pi
    ia = jnp.arange(n1, dtype=jnp.int32)
    ic = jnp.arange(nc, dtype=jnp.int32)
    ib = jnp.arange(DFT_MINOR, dtype=jnp.int32)
    m = (ic[None, :, None] * (DFT_MINOR * ia[None, None, :] + ib[:, None, None])) % n
    ang = m.astype(F32) * (two_pi / n)
    g = jnp.concatenate([jnp.cos(ang), -jnp.sin(ang)], axis=1).astype(BF16)
    th = ((ib[:, None] * ib[None, :]) % DFT_MINOR).astype(F32) * (two_pi / DFT_MINOR)
    c, s = jnp.cos(th), jnp.sin(th)
    mf = jnp.concatenate([jnp.concatenate([c, s], 1), jnp.concatenate([-s, c], 1)], 0).astype(BF16)
    kk = ic[:, None, None] + n1 * ib[None, None, :]
    mi = (ib[None, :, None] * kk) % n
    ps = mi.astype(F32) * (two_pi / n)
    cr, ci = jnp.cos(ps), jnp.sin(ps)
    cinv = jnp.concatenate([jnp.concatenate([cr, -ci], 2), jnp.concatenate([ci, cr], 2)], 1).astype(BF16)
    ph = ((ia[: n1 // 2, None] * ic[None, :]) % n1).astype(F32) * (two_pi / n1)
    wc = jnp.where((ic == 0) | (ic == n1 // 2), 1.0, jnp.where(ic < n1 // 2, 2.0, 0.0)) * (1.0 / n)
    m3 = (jnp.concatenate([jnp.cos(ph), -jnp.sin(ph)], 1) * jnp.tile(wc, 2)[None, :]).astype(BF16)
    return g, mf, cinv, m3


def _s1_kernel(g_ref, x_ref, o_ref):
    for j in range(SUB):
        o_ref[:, j, :] = _dot(g_ref[j], x_ref[:, j, :].astype(BF16))


def dft_stage1(x4, g, comp, rows, wdt, name):
    gsz = x4.shape[0]
    n1x2 = g.shape[1]
    return pl.pallas_call(
        _s1_kernel, grid=(gsz, DFT_MINOR // SUB),
        in_specs=[pl.BlockSpec((SUB, n1x2, rows), lambda i, bb: (bb, 0, 0)),
                  pl.BlockSpec((None, rows, SUB, wdt), lambda i, bb: (i, 0, bb, comp))],
        out_specs=pl.BlockSpec((None, n1x2, SUB, wdt), lambda i, bb: (i, 0, bb, 0)),
        out_shape=jax.ShapeDtypeStruct((gsz, n1x2, DFT_MINOR, wdt), F32),
        compiler_params=_cp("arbitrary", "arbitrary"), name=name,
    )(g, x4)


def _s2f_kernel(mf_ref, a_ref, o_ref):
    wdt = a_ref.shape[-1]
    for j in range(SUB):
        y = _dot(mf_ref[...], a_ref[:, j].astype(BF16).reshape(2 * DFT_MINOR, wdt))
        o_ref[:, j] = y.reshape(2, DFT_MINOR, wdt)


def filter_spectrum(a5, mf):
    no, _, n1, _, wdt = a5.shape
    ct = 512
    spec = pl.BlockSpec((None, 2, SUB, DFT_MINOR, ct), lambda o, c, jc: (o, 0, c, 0, jc))
    return pl.pallas_call(
        _s2f_kernel, grid=(no, n1 // SUB, wdt // ct),
        in_specs=[pl.BlockSpec((2 * DFT_MINOR, 2 * DFT_MINOR), lambda o, c, jc: (0, 0)), spec],
        out_specs=spec,
        out_shape=jax.ShapeDtypeStruct(a5.shape, F32),
        compiler_params=_cp("arbitrary", "arbitrary", "arbitrary"), name="hyena_filter_spectrum",
    )(mf, a5)


def _s2_kernel(mf_ref, ci_ref, h_ref, a_ref, o_ref):
    wdt = a_ref.shape[-1]
    for j in range(SUB):
        y = _dot(mf_ref[...], a_ref[:, j].astype(BF16).reshape(2 * DFT_MINOR, wdt))
        yr, yi = y[:DFT_MINOR], y[DFT_MINOR:]
        hr, hi = h_ref[0, j], h_ref[1, j]
        z = jnp.concatenate([yr * hr - yi * hi, yr * hi + yi * hr], axis=0).astype(BF16)
        o_ref[:, j, :] = _dot(ci_ref[j], z)


def spectral_multiply(a5, spec5, order, mf, cinv):
    bsz, _, n1, _, wdt = a5.shape
    ct = 512
    return pl.pallas_call(
        _s2_kernel, grid=(n1 // SUB, bsz, wdt // ct),
        in_specs=[pl.BlockSpec((2 * DFT_MINOR, 2 * DFT_MINOR), lambda c, bi, jc: (0, 0)),
                  pl.BlockSpec((SUB, 2 * DFT_MINOR, 2 * DFT_MINOR), lambda c, bi, jc: (c, 0, 0)),
                  pl.BlockSpec((None, 2, SUB, DFT_MINOR, ct), lambda c, bi, jc: (order, 0, c, 0, jc)),
                  pl.BlockSpec((None, 2, SUB, DFT_MINOR, ct), lambda c, bi, jc: (bi, 0, c, 0, jc))],
        out_specs=pl.BlockSpec((None, 2 * DFT_MINOR, SUB, ct), lambda c, bi, jc: (bi, 0, c, jc)),
        out_shape=jax.ShapeDtypeStruct((bsz, 2 * DFT_MINOR, n1, wdt), F32),
        compiler_params=_cp("arbitrary", "arbitrary", "arbitrary"), name="hyena_spectral_multiply",
    )(mf, cinv, spec5, a5)


def _s3_mid_kernel(m3_ref, g_ref, bq_ref, z_ref, x_ref, sk_ref, zo_ref, ao_ref):
    wdt = bq_ref.shape[-1]
    for j in range(SUB):
        y = _dot(m3_ref[...], bq_ref[:, j].astype(BF16).reshape(-1, wdt))
        z = x_ref[:, j, :] * (y + z_ref[:, j, :] * sk_ref[...])
        zo_ref[:, j, :] = z
        ao_ref[:, j, :] = _dot(g_ref[j], z.astype(BF16))


def _s3_last_kernel(m3_ref, bq_ref, z_ref, x_ref, sk_ref, gate_ref, o_ref):
    wdt = bq_ref.shape[-1]
    for j in range(SUB):
        y = _dot(m3_ref[...], bq_ref[:, j].astype(BF16).reshape(-1, wdt))
        z = x_ref[:, j, :] * (y + z_ref[:, j, :] * sk_ref[...])
        o_ref[:, j, :] = z * _silu(gate_ref[:, j, :])


def _hyena_ctx_kernel(ff_ref, fd_ref, fi_ref, circ_ref, v_ref, x1_ref, x2_ref, gate_ref, sk_ref, o_ref, *, nc):
    z = v_ref[...].astype(F32)
    xs = (x1_ref, x2_ref)
    for o in range(2):
        hs = _dot(ff_ref[...], circ_ref[o].astype(BF16))
        us = _dot(fd_ref[...], z.astype(BF16))
        hr, hi = hs[:nc], hs[nc:]
        ur, ui = us[:nc], us[nc:]
        zz = jnp.concatenate([ur * hr - ui * hi, ur * hi + ui * hr], axis=0).astype(BF16)
        y = _dot(fi_ref[...], zz)
        z = xs[o][...].astype(F32) * (y + z * sk_ref[o:o + 1, :])
    o_ref[...] = (z * _silu(gate_ref[...].astype(F32))).astype(o_ref.dtype)


def hyena_context(hv, p, circ_c, skip, n_lat_rows, lc):
    bsz, _, w3 = hv.shape
    wdt = w3 // 3
    nc = 2 * lc
    ct = 256
    two_pi = 2.0 * math.pi
    ik = jnp.arange(nc, dtype=jnp.int32)
    th = ((ik[:, None] * ik[None, :]) % nc).astype(F32) * (two_pi / nc)
    c, s = jnp.cos(th), jnp.sin(th)
    ffull = jnp.concatenate([c, -s], axis=0).astype(BF16)
    fdata = ffull[:, :lc]
    finv = (jnp.concatenate([c[:lc], -s[:lc]], axis=1) * (1.0 / nc)).astype(BF16)
    rb = n_lat_rows // lc
    cw = wdt // ct
    full = lambda shape: pl.BlockSpec(shape, lambda bi, j: (0,) * len(shape))
    return pl.pallas_call(
        functools.partial(_hyena_ctx_kernel, nc=nc),
        grid=(bsz, cw),
        in_specs=[full((2 * nc, nc)), full((2 * nc, lc)), full((lc, 2 * nc)),
                  pl.BlockSpec((2, nc, ct), lambda bi, j: (0, 0, j)),
                  pl.BlockSpec((None, lc, ct), lambda bi, j: (bi, rb, j)),
                  pl.BlockSpec((None, lc, ct), lambda bi, j: (bi, rb, cw + j)),
                  pl.BlockSpec((None, lc, ct), lambda bi, j: (bi, rb, 2 * cw + j)),
                  pl.BlockSpec((None, lc, ct), lambda bi, j: (bi, rb, C_HYGATE // ct + j)),
                  pl.BlockSpec((2, ct), lambda bi, j: (0, j))],
        out_specs=pl.BlockSpec((None, lc, ct), lambda bi, j: (bi, 0, j)),
        out_shape=jax.ShapeDtypeStruct((bsz, lc, wdt), BF16),
        compiler_params=_cp("arbitrary", "arbitrary"), name="hyena_context",
    )(ffull, fdata, finv, circ_c, hv, hv, hv, p, skip)


def hyena_latent(hv, p, circ_l, skip, n_lat_rows, tables):
    g, mf, cinv, m3 = tables
    bsz, t, w3 = hv.shape
    wdt = w3 // 3
    n1 = 2 * n_lat_rows // DFT_MINOR
    nc = _dft_num_c(n1)
    half = n1 // 2
    ct = 512
    nj = wdt // ct
    af = dft_stage1(circ_l.reshape(2, n1, DFT_MINOR, wdt), g, 0, n1, wdt, "hyena_filter_stage1")
    spec5 = filter_spectrum(af.reshape(2, 2, nc, DFT_MINOR, wdt), mf)
    gd = g[:, :, :half]
    hv4 = hv.reshape(bsz, t // DFT_MINOR, DFT_MINOR, w3)
    p4 = p.reshape(bsz, t // DFT_MINOR, DFT_MINOR, p.shape[-1])
    sk = skip.reshape(2, 1, wdt)
    a = dft_stage1(hv4, gd, 0, half, wdt, "hyena_stage1")
    bq = spectral_multiply(a.reshape(bsz, 2, nc, DFT_MINOR, wdt), spec5, 0, mf, cinv)
    grid = (bsz, DFT_MINOR // SUB, nj)
    nat = lambda cb: pl.BlockSpec((None, half, SUB, ct), lambda bi, bb, jc: (bi, 0, bb, cb + jc))
    bq_spec = pl.BlockSpec((None, 2, SUB, nc, ct), lambda bi, bb, jc: (bi, 0, bb, 0, jc))
    m3_spec = pl.BlockSpec((half, 2 * nc), lambda bi, bb, jc: (0, 0))
    sk_spec = lambda o: pl.BlockSpec((None, 1, ct), lambda bi, bb, jc: (o, 0, jc))
    z1, a = pl.pallas_call(
        _s3_mid_kernel, grid=grid,
        in_specs=[m3_spec, pl.BlockSpec((SUB, 2 * nc, half), lambda bi, bb, jc: (bb, 0, 0)),
                  bq_spec, nat(0), nat(nj), sk_spec(0)],
        out_specs=[nat(0), pl.BlockSpec((None, 2 * nc, SUB, ct), lambda bi, bb, jc: (bi, 0, bb, jc))],
        out_shape=[jax.ShapeDtypeStruct((bsz, half, DFT_MINOR, wdt), F32),
                   jax.ShapeDtypeStruct((bsz, 2 * nc, DFT_MINOR, wdt), F32)],
        compiler_params=_cp("arbitrary", "arbitrary", "arbitrary"), name="hyena_stage3_mid",
    )(m3, gd, bq.reshape(bsz, 2, DFT_MINOR, nc, wdt), hv4, hv4, sk)
    bq = spectral_multiply(a.reshape(bsz, 2, nc, DFT_MINOR, wdt), spec5, 1, mf, cinv)
    y = pl.pallas_call(
        _s3_last_kernel, grid=grid,
        in_specs=[m3_spec, bq_spec, nat(0), nat(2 * nj), sk_spec(1), nat(C_HYGATE // ct)],
        out_specs=nat(0),
        out_shape=jax.ShapeDtypeStruct((bsz, half, DFT_MINOR, wdt), F32),
        compiler_params=_cp("arbitrary", "arbitrary", "arbitrary"), name="hyena_stage3_last",
    )(m3, bq.reshape(bsz, 2, DFT_MINOR, nc, wdt), z1, hv4, sk, p4)
    return y.reshape(bsz, n_lat_rows, wdt)


SCAN_ROWS = 256


def _scan_block_map(direction, n_lat_rows, t_rows):
    n_lat, n_all = n_lat_rows // SCAN_ROWS, t_rows // SCAN_ROWS
    if direction == 0:
        return lambda t: (t + n_lat) % n_all
    return lambda t: n_all - 1 - t


def _sub_chunks(rows, chunk, direction):
    order = range(rows // chunk) if direction == 0 else reversed(range(rows // chunk))
    return [slice(i * chunk, (i + 1) * chunk) for i in order]


def _tri_mask(n, direction):
    ri = lax.broadcasted_iota(jnp.int32, (n, n), 0)
    ci = lax.broadcasted_iota(jnp.int32, (n, n), 1)
    return (ci <= ri) if direction == 0 else (ci >= ri)


def _expand_heads(v, lane0):
    rows = v.shape[0]
    low_half = lax.broadcasted_iota(jnp.int32, (1, LANE), 1) < SSD_HEADDIM
    tiles = []
    for k in range(SSD_HEADS // 2):
        lo = jnp.broadcast_to(v[:, lane0 + 2 * k:lane0 + 2 * k + 1], (rows, LANE))
        hi = jnp.broadcast_to(v[:, lane0 + 2 * k + 1:lane0 + 2 * k + 2], (rows, LANE))
        tiles.append(jnp.where(low_half, lo, hi))
    return jnp.concatenate(tiles, axis=1)


def _ssd_kernel(*refs, direction):
    d = direction
    if d == 0:
        xs_ref, bm_ref, cm_ref, sm_ref, dtb_ref, alog_ref, o_ref, st_ref = refs
    else:
        (xs_ref, bm_ref, cm_ref, sm_ref, dtb_ref, alog_ref,
         y0_ref, z_ref, dsk_ref, nw_ref, o_ref, st_ref) = refs
    q = SSD_CHUNK
    hd = SSD_HEADDIM
    gw = SSD_RPG * hd

    @pl.when(pl.program_id(1) == 0)
    def _():
        st_ref[...] = jnp.zeros_like(st_ref)

    tri = _tri_mask(q, d)
    trib = jnp.where(tri, 1.0, 0.0).astype(BF16)
    neg_a = -jnp.exp(alog_ref[...])
    for rs in _sub_chunks(xs_ref.shape[0], q, d):
        xs = xs_ref[rs, :].astype(F32)
        bm = bm_ref[rs, :].astype(BF16)
        cm = cm_ref[rs, :].astype(BF16)
        dt = _softplus(sm_ref[rs, :] + dtb_ref[...])
        a = dt * neg_a
        acum = _masked_cumsum(trib, a)
        acum_t = acum.T
        dtx = _expand_heads(dt, SSD_HEADS * d)
        ax = _expand_heads(acum, SSD_HEADS * d)
        atx = ax[q - 1:q, :] if d == 0 else ax[0:1, :]
        xdt = xs * dtx
        e_in = jnp.exp(ax)
        xw = (xdt * jnp.exp(atx - ax)).astype(BF16)
        xdt_b = xdt.astype(BF16)
        ys = []
        for g in range(SSD_GROUPS):
            bg = bm[:, g * SSD_STATE:(g + 1) * SSD_STATE]
            cg = cm[:, g * SSD_STATE:(g + 1) * SSD_STATE]
            sc = _nt(cg, bg)
            st_g = st_ref[g * gw:(g + 1) * gw, :]
            y_int = _nt(cg, st_g.astype(BF16))
            yh, decs = [], []
            for r in range(SSD_RPG):
                hl = SSD_HEADS * d + SSD_RPG * g + r
                seg = acum[:, hl:hl + 1] - acum_t[hl:hl + 1, :]
                dm = jnp.exp(jnp.where(tri, seg, -1e30))
                ch = (SSD_RPG * g + r) * hd
                yh.append(_dot((sc * dm).astype(BF16), xdt_b[:, ch:ch + hd]))
                a_tot = acum[q - 1:q, hl:hl + 1] if d == 0 else acum[0:1, hl:hl + 1]
                decs.append(jnp.broadcast_to(jnp.exp(a_tot), (hd, SSD_STATE)))
            ys.append(jnp.concatenate(yh, axis=1) + y_int * e_in[:, g * gw:(g + 1) * gw])
            st_ref[g * gw:(g + 1) * gw, :] = (st_g * jnp.concatenate(decs, axis=0)
                                              + _tn(xw[:, g * gw:(g + 1) * gw], bg))
        y = jnp.concatenate(ys, axis=1)
        if d == 0:
            o_ref[rs, :] = y
        else:
            y = (y0_ref[rs, :] + y + dsk_ref[...] * xs) * _silu(z_ref[rs, :].astype(F32))
            parts = []
            for g in range(SSD_GROUPS):
                yg = y[:, g * gw:(g + 1) * gw]
                parts.append(yg * lax.rsqrt(jnp.mean(yg * yg, axis=-1, keepdims=True) + EPS))
            o_ref[rs, :] = (jnp.concatenate(parts, axis=1) * nw_ref[...]).astype(o_ref.dtype)


def ssd_mixer(xbc, p, psmall, dtb, alog, dskip, normw, n_lat_rows):
    bsz, t, _ = xbc.shape
    q = SCAN_ROWS
    w = BRANCH_W
    nct = t // q
    outs = None
    for d in range(2):
        cm_ = _scan_block_map(d, n_lat_rows, t)
        row = lambda width, cb, cm_=cm_: pl.BlockSpec((None, q, width), lambda bi, ti: (bi, cm_(ti), cb))
        const = lambda shape: pl.BlockSpec(shape, lambda bi, ti: (0,) * len(shape))
        in_specs = [row(w, 0), row(4 * SSD_STATE, 2), row(4 * SSD_STATE, 3), row(NSMALL, 0),
                    const((1, NSMALL)), const((1, NSMALL))]
        args = [xbc, xbc, xbc, psmall, dtb, alog]
        if d == 1:
            in_specs += [row(w, 0), row(w, C_SSDZ // w), const((1, w)), const((1, w))]
            args += [outs, p, dskip, normw]
        outs = pl.pallas_call(
            functools.partial(_ssd_kernel, direction=d), grid=(bsz, nct),
            in_specs=in_specs, out_specs=row(w, 0),
            out_shape=jax.ShapeDtypeStruct((bsz, t, w), F32 if d == 0 else BF16),
            scratch_shapes=[pltpu.VMEM((SSD_HEADS * SSD_HEADDIM, SSD_STATE), F32)],
            compiler_params=_cp("arbitrary", "arbitrary"), name=f"ssd_scan_dir{d}",
        )(*args)
    return outs


def _gla_kernel(*refs, direction):
    d = direction
    if d == 0:
        q_ref, k_ref, v_ref, sm_ref, w2_ref, b2_ref, o_ref, st_ref = refs
    else:
        q_ref, k_ref, v_ref, sm_ref, w2_ref, b2_ref, y0_ref, g_ref, nw_ref, o_ref, st_ref = refs
    cs = GLA_CHUNK
    dk, dv = GLA_HDK, GLA_HDV

    @pl.when(pl.program_id(1) == 0)
    def _():
        st_ref[...] = jnp.zeros_like(st_ref)

    tri = _tri_mask(cs, d)
    trib = jnp.where(tri, 1.0, 0.0).astype(BF16)
    mid = cs // 2 if d == 0 else cs - 1 - cs // 2
    for rs in _sub_chunks(q_ref.shape[0], cs, d):
        logit = _dot(sm_ref[rs, :].astype(BF16), w2_ref[...]) + b2_ref[...]
        gl = _log_sigmoid(logit) * (1.0 / GLA_NORMALIZER)
        gc = _masked_cumsum(trib, gl)
        g_mid = gc[mid:mid + 1, :]
        g_last = gc[cs - 1:cs, :] if d == 0 else gc[0:1, :]
        qf = q_ref[rs, :].astype(F32) * (dk ** -0.5)
        kf = k_ref[rs, :].astype(F32)
        vb = v_ref[rs, :].astype(BF16)
        qa = (qf * jnp.exp(gc - g_mid)).astype(BF16)
        ka = (kf * jnp.exp(g_mid - gc)).astype(BF16)
        qs = (qf * jnp.exp(gc)).astype(BF16)
        ke = (kf * jnp.exp(g_last - gc)).astype(BF16)
        dec = jnp.exp(g_last)
        ys = []
        for h in range(GLA_HEADS):
            ks, vs = slice(h * dk, (h + 1) * dk), slice(h * dv, (h + 1) * dv)
            att = jnp.where(tri, _nt(qa[:, ks], ka[:, ks]), 0.0)
            st_h = st_ref[h]
            ys.append(_dot(att.astype(BF16), vb[:, vs]) + _nt(qs[:, ks], st_h.astype(BF16)))
            st_ref[h] = st_h * dec[:, ks] + _tn(vb[:, vs], ke[:, ks])
        y = jnp.concatenate(ys, axis=1)
        if d == 0:
            o_ref[rs, :] = y
        else:
            y = y0_ref[rs, :] + y
            parts = []
            for h in range(GLA_HEADS):
                yh = y[:, h * dv:(h + 1) * dv]
                parts.append(yh * lax.rsqrt(jnp.mean(yh * yh, axis=-1, keepdims=True) + EPS))
            y = jnp.concatenate(parts, axis=1) * nw_ref[...]
            o_ref[rs, :] = (y * _silu(g_ref[rs, :].astype(F32))).astype(o_ref.dtype)


def gla_mixer(p, psmall, w2p, b2, normw, n_lat_rows):
    bsz, t, _ = p.shape
    cs = SCAN_ROWS
    w = BRANCH_W
    hw = GLA_HEADS * GLA_HDK
    nct = t // cs
    outs = None
    for d in range(2):
        cm_ = _scan_block_map(d, n_lat_rows, t)
        row = lambda width, cb, cm_=cm_: pl.BlockSpec((None, cs, width), lambda bi, ti: (bi, cm_(ti), cb))
        const = lambda shape: pl.BlockSpec(shape, lambda bi, ti: (0,) * len(shape))
        in_specs = [row(hw, C_GQ // hw), row(hw, C_GK // hw), row(w, C_GV // w), row(NSMALL, 0),
                    pl.BlockSpec((None, NSMALL, hw), lambda bi, ti, d=d: (d, 0, 0)),
                    pl.BlockSpec((None, 1, hw), lambda bi, ti, d=d: (d, 0, 0))]
        args = [p, p, p, psmall, w2p, b2]
        if d == 1:
            in_specs += [row(w, 0), row(w, C_GG // w), const((1, w))]
            args += [outs, p, normw]
        outs = pl.pallas_call(
            functools.partial(_gla_kernel, direction=d), grid=(bsz, nct),
            in_specs=in_specs, out_specs=row(w, 0),
            out_shape=jax.ShapeDtypeStruct((bsz, t, w), F32 if d == 0 else BF16),
            scratch_shapes=[pltpu.VMEM((GLA_HEADS, GLA_HDV, GLA_HDK), F32)],
            compiler_params=_cp("arbitrary", "arbitrary"), name=f"gla_scan_dir{d}",
        )(*args)
    return outs


def _rope(x, cos, sin_signed):
    half = RET_HD // 2
    lo, hi = x[:, :half], x[:, half:]
    lo = lo * cos[:, :half] + pltpu.roll(lo, half // 2, 1) * sin_signed[:, :half]
    hi = hi * cos[:, half:] + pltpu.roll(hi, half // 2, 1) * sin_signed[:, half:]
    return jnp.concatenate([lo, hi], axis=1)


def _ret_kernel(*refs, direction):
    d = direction
    if d == 0:
        q_ref, k_ref, v_ref, cos_ref, sin_ref, dr_ref, o_ref, st_ref = refs
    else:
        q_ref, k_ref, v_ref, cos_ref, sin_ref, dr_ref, y0_ref, g_ref, o_ref, st_ref = refs
    cs = RET_CHUNK
    hdim = RET_HD

    @pl.when(pl.program_id(1) == 0)
    def _():
        st_ref[...] = jnp.zeros_like(st_ref)

    lam_all = -jnp.exp(dr_ref[...])
    tri = _tri_mask(cs, d)
    ri = lax.broadcasted_iota(jnp.int32, (cs, cs), 0)
    ci = lax.broadcasted_iota(jnp.int32, (cs, cs), 1)
    lag = jnp.abs(ri - ci).astype(F32)
    pos = lax.broadcasted_iota(jnp.int32, (cs, hdim), 0).astype(F32)
    steps_in = (pos + 1.0) if d == 0 else (cs - pos)
    steps_out = (cs - 1.0 - pos) if d == 0 else pos
    lams = [lam_all[d:d + 1, h:h + 1] for h in range(RET_HEADS)]
    dms = [jnp.where(tri, jnp.exp(lam * lag), 0.0) for lam in lams]
    w_in = [jnp.exp(lam * steps_in) for lam in lams]
    w_out = [jnp.exp(lam * steps_out) for lam in lams]
    w_chunk = [jnp.exp(lam * cs) for lam in lams]
    for rs in _sub_chunks(q_ref.shape[0], cs, d):
        cos, sin_s = cos_ref[rs, :], sin_ref[rs, :]
        for h in range(RET_HEADS):
            hs = slice(h * hdim, (h + 1) * hdim)
            qh = _rope(q_ref[rs, hs].astype(F32), cos, sin_s)
            kh = _rope(k_ref[rs, hs].astype(F32) * (hdim ** -0.5), cos, sin_s)
            vb = v_ref[rs, hs].astype(BF16)
            qb = qh.astype(BF16)
            att = (_nt(qb, kh.astype(BF16)) * dms[h]).astype(BF16)
            st_h = st_ref[h]
            y = _dot(att, vb) + _dot(qb, st_h.astype(BF16)) * w_in[h]
            st_ref[h] = st_h * w_chunk[h] + _tn((kh * w_out[h]).astype(BF16), vb)
            if d == 0:
                o_ref[rs, hs] = y
            else:
                y = _ln_rows(y0_ref[rs, hs] + y)
                o_ref[rs, hs] = (y * _silu(g_ref[rs, hs].astype(F32))).astype(o_ref.dtype)


def ret_mixer(p, cos_t, sin_t, decay_pad, n_lat_rows):
    bsz, t, _ = p.shape
    cs = SCAN_ROWS
    w = BRANCH_W
    nct = t // cs
    outs = None
    for d in range(2):
        cm_ = _scan_block_map(d, n_lat_rows, t)
        row = lambda cb, cm_=cm_: pl.BlockSpec((None, cs, w), lambda bi, ti: (bi, cm_(ti), cb))
        tab = pl.BlockSpec((cs, RET_HD), lambda bi, ti, cm_=cm_: (cm_(ti), 0))
        in_specs = [row(C_RQ // w), row(C_RK // w), row(C_RV // w), tab, tab,
                    pl.BlockSpec((8, LANE), lambda bi, ti: (0, 0))]
        args = [p, p, p, cos_t, sin_t, decay_pad]
        if d == 1:
            in_specs += [row(0), row(C_RG // w)]
            args += [outs, p]
        outs = pl.pallas_call(
            functools.partial(_ret_kernel, direction=d), grid=(bsz, nct),
            in_specs=in_specs, out_specs=row(0),
            out_shape=jax.ShapeDtypeStruct((bsz, t, w), F32 if d == 0 else BF16),
            scratch_shapes=[pltpu.VMEM((RET_HEADS, RET_HD, RET_HD), F32)],
            compiler_params=_cp("arbitrary", "arbitrary"), name=f"ret_scan_dir{d}",
        )(*args)
    return outs


def _rope_tables(n_lat_rows, lc):
    half = RET_HD // 2
    inv = ROPE_BASE ** (-jnp.arange(0, half, 2, dtype=F32) / half)
    tpos = jnp.arange(n_lat_rows)
    row = (tpos // GRID_W).astype(F32)[:, None] * inv[None, :]
    col = (tpos % GRID_W).astype(F32)[:, None] * inv[None, :]
    cos = jnp.concatenate([jnp.cos(row)] * 2 + [jnp.cos(col)] * 2, axis=1)
    sin = jnp.concatenate([-jnp.sin(row), jnp.sin(row), -jnp.sin(col), jnp.sin(col)], axis=1)
    cos = jnp.concatenate([cos, jnp.ones((lc, RET_HD), F32)], axis=0)
    sin = jnp.concatenate([sin, jnp.zeros((lc, RET_HD), F32)], axis=0)
    return cos, sin


def _pad_lanes(v, start):
    v = v.reshape(-1).astype(F32)
    return jnp.zeros((1, NSMALL), F32).at[0, start:start + v.shape[0]].set(v)


def kernel(x, c, ctx, c_ctx, w_ada, b_ada, w_in, hy_conv_w, hy_conv_b, hy_w1, hy_b1, hy_w2, hy_b2, hy_w3,
           hy_b3, hy_w4, hy_freq, hy_skip, ssd_conv_w, ssd_conv_b, ssd_a_log, ssd_dt_bias, ssd_d, ssd_norm_w,
           gla_w2, gla_b2, gla_norm_w, ret_decay, w_gate, w_br, w_out, ln_g, ln_b):
    bsz, n_lat, d = x.shape
    lc = ctx.shape[1]
    t = n_lat + lc
    depth = w_in.shape[0]
    w = BRANCH_W
    assert bsz <= 2 and d == D_MODEL and n_lat % 256 == 0 and lc % 256 == 0

    cs = jnp.zeros((8, d), F32).at[:bsz].set(c).at[2].set(c_ctx)
    mod = ada_modulation(cs, w_ada, b_ada).reshape(depth, 8, 1, 3 * d)

    cos_t, sin_t = _rope_tables(n_lat, lc)
    tables = _dft_tables(2 * n_lat // DFT_MINOR)
    deltas = jnp.abs(jnp.linspace(math.log(HY_TARGET) / HY_SLOW, math.log(HY_TARGET) / HY_FAST, w,
                                  dtype=F32)).reshape(1, w)

    w_gate_b, w_br_b, w_out_b = w_gate.astype(BF16), w_br.astype(BF16), w_out.astype(BF16)
    w_in_t = jnp.swapaxes(w_in, 1, 2).astype(BF16)
    w_small = jnp.concatenate([w_in_t[:, W_IN_DT:W_IN_DT + SMALL_W], w_in_t[:, W_IN_LR:W_IN_LR + SMALL_W],
                               jnp.zeros((depth, NSMALL - 64, d), BF16)], axis=1)
    s, h = ln_modulate(x, mod[0], None, None, n_lat, pre_ln=False, emit_h=True, ctx=ctx)
    for l in range(depth):
        h2 = h.reshape(bsz * t, d)
        p = matmul_nt(h2, w_in_t, l, F32, "in_projection", n_out=NP,
                      row_start=_main_projection_row).reshape(bsz, t, NP)
        psmall = matmul_nt(h2, w_small, l, F32, "in_projection_small").reshape(bsz, t, NSMALL)

        hv = short_conv(p, C_HYIN, 3 * w, hy_conv_w[l], hy_conv_b[l], n_lat, False, F32, "hyena_short_conv")
        fargs = (hy_w1[l], hy_b1[l], hy_w2[l], hy_b2[l], hy_w3[l], hy_b3[l], hy_freq[l], hy_w4[l], deltas)
        circ_l = hyena_filter(n_lat, *fargs)
        circ_c = hyena_filter(lc, *fargs)
        y_hy = jnp.concatenate([hyena_latent(hv, p, circ_l, hy_skip[l], n_lat, tables).astype(BF16),
                                hyena_context(hv, p, circ_c, hy_skip[l], n_lat, lc)], axis=1)

        xbc = short_conv(p, C_XBC, 2 * w, ssd_conv_w[l], ssd_conv_b[l], n_lat, True, F32, "ssd_short_conv")
        y_ssd = ssd_mixer(xbc, p, psmall, _pad_lanes(ssd_dt_bias[l], 0), _pad_lanes(ssd_a_log[l], 0),
                          jnp.repeat(ssd_d[l].astype(F32), SSD_HEADDIM).reshape(1, w),
                          ssd_norm_w[l].astype(F32).reshape(1, w), n_lat)

        hw = GLA_HEADS * GLA_HDK
        w2p = jnp.zeros((2, NSMALL, hw), F32)
        for dd in range(2):
            w2p = w2p.at[dd, SMALL_W + GLA_RANK * dd:SMALL_W + GLA_RANK * (dd + 1)].set(gla_w2[l, dd])
        y_gla = gla_mixer(p, psmall, w2p.astype(BF16), gla_b2[l].astype(F32).reshape(2, 1, hw),
                          jnp.tile(gla_norm_w[l].astype(F32), GLA_HEADS).reshape(1, w), n_lat)

        decay_pad = jnp.zeros((8, LANE), F32).at[:2, :RET_HEADS].set(ret_decay[l])
        y_ret = ret_mixer(p, cos_t, sin_t, decay_pad, n_lat)

        ys = [y.reshape(bsz * t, w) for y in (y_hy, y_ssd, y_gla, y_ret)]
        m = gated_merge(h2, ys, w_gate_b, w_br_b, l)
        gate_rows = mod[l][:, :, 2 * d:3 * d]
        pre = out_projection(m, w_out_b, l, s.reshape(bsz * t, d), gate_rows, t, n_lat)
        pre = pre.reshape(bsz, t, d)
        if l + 1 < depth:
            s, h = ln_modulate(pre, mod[l + 1], ln_g[l], ln_b[l], n_lat, pre_ln=True, emit_h=True)
        else:
            s = ln_modulate(pre, None, ln_g[l], ln_b[l], n_lat, pre_ln=True, emit_h=False, rows=n_lat)[0]
    return s
```
